```python
import jax
import jax.numpy as jnp
from jax import lax
import numpy as np

D_MODEL = 1024
BATCH = 8
SEQ = 4096
DEPTH = 4

HEAD_DIM = 64
D_MIX = D_MODEL
N_MIXERS = 4
GROUP_HEADS = D_MIX // (N_MIXERS * HEAD_DIM)
GROUP_W = GROUP_HEADS * HEAD_DIM
QBLK = 128
EPS = 1e-6

DIL_PAIRS = ((128, 1), (512, 4), (2048, 16))

NSA_KV = 2
NSA_CMP_L = 32
NSA_CMP_D = 16
NSA_CMP_HID = 2 * HEAD_DIM
NSA_SLC_L = 64
NSA_TOP_N = 16
NSA_WIN = 512
NSA_QC = 32
NSA_FORCE = 1e4

GLA_DK = HEAD_DIM // 2
GLA_RANK = 16
GLA_TAU = 16.0
GLA_CHUNK = 64

SWA_WIN = 128
SWA_KV = 1

IN_SPLITS = (
    ('a_q', GROUP_W), ('a_k', GROUP_W), ('a_v', GROUP_W), ('a_z', GROUP_W),
    ('b_q', GROUP_W), ('b_kc', NSA_KV * HEAD_DIM), ('b_vc', NSA_KV * HEAD_DIM),
    ('b_ks', NSA_KV * HEAD_DIM), ('b_vs', NSA_KV * HEAD_DIM),
    ('b_kw', NSA_KV * HEAD_DIM), ('b_vw', NSA_KV * HEAD_DIM),
    ('b_g', GROUP_HEADS * 3), ('b_z', GROUP_W),
    ('c_q', GROUP_HEADS * GLA_DK), ('c_k', GROUP_HEADS * GLA_DK), ('c_v', GROUP_W),
    ('c_a', GLA_RANK), ('c_z', GROUP_W),
    ('d_q', GROUP_W), ('d_k', SWA_KV * HEAD_DIM), ('d_v', SWA_KV * HEAD_DIM), ('d_z', GROUP_W),
)
D_IN = sum(w for _, w in IN_SPLITS)

kernel_name = 'hybrid_dilated_nsa_gla_swa_block'


def rms_norm(x, g):
    xf = x.astype(jnp.float32)
    y = xf * lax.rsqrt(jnp.mean(xf * xf, axis=-1, keepdims=True) + EPS)
    return (y * g.astype(jnp.float32)).astype(x.dtype)


def split_columns(u):
    out = {}
    off = 0
    for name, w in IN_SPLITS:
        out[name] = u[..., off:off + w]
        off += w
    return out


def banded_attention(q, k, v, back, sink=None):
    n, L, hq, dh = q.shape
    hkv = k.shape[2]
    g = hq // hkv
    blk = min(QBLK, L)
    nq = -(-L // blk)
    lp = nq * blk
    nb = -(-back // blk)
    span = (nb + 1) * blk
    pad_end = ((0, 0), (0, lp - L), (0, 0), (0, 0))
    q = jnp.pad(q, pad_end)
    pad_kv = ((0, 0), (nb * blk, lp - L), (0, 0), (0, 0))
    kp = jnp.pad(k, pad_kv).reshape(n, nq + nb, blk, hkv, dh)
    vp = jnp.pad(v, pad_kv).reshape(n, nq + nb, blk, hkv, dh)
    bidx = jnp.arange(nq)[:, None] + jnp.arange(nb + 1)[None, :]
    kb = kp[:, bidx].reshape(n, nq, span, hkv, dh)
    vb = vp[:, bidx].reshape(n, nq, span, hkv, dh)
    qb = q.reshape(n, nq, blk, hkv, g, dh)
    s = jnp.einsum('bnqhgd,bnkhd->bnhgqk', qb, kb, preferred_element_type=jnp.float32) * (dh ** -0.5)
    qpos = jnp.arange(lp).reshape(nq, blk)
    kpos = (jnp.arange(nq)[:, None] - nb) * blk + jnp.arange(span)[None, :]
    diff = qpos[:, :, None] - kpos[:, None, :]
    mask = (diff >= 0) & (diff <= back) & (kpos[:, None, :] >= 0)
    s = jnp.where(mask[None, :, None, None], s, -jnp.inf)
    m = jnp.max(s, axis=-1, keepdims=True)
    if sink is not None:
        sk = sink.astype(jnp.float32).reshape(hkv, g)[None, None, :, :, None, None]
        m = jnp.maximum(m, sk)
    p = jnp.exp(s - m)
    l = jnp.sum(p, axis=-1, keepdims=True)
    denom = l if sink is None else l + jnp.exp(sk - m)
    o = jnp.einsum('bnhgqk,bnkhd->bnhgqd', p, vb.astype(jnp.float32)) / denom
    o = o.transpose(0, 1, 4, 2, 3, 5).reshape(n, lp, hq, dh)[:, :L]
    lse = (m + jnp.log(l))[..., 0].transpose(0, 1, 4, 2, 3).reshape(n, lp, hq)[:, :L]
    return o, lse


def dilated_attention(q, k, v):
    b, s, h, dh = q.shape
    outs, lses = [], []
    for w, r in DIL_PAIRS:
        def to_res(t):
            return t.reshape(b, s // r, r, h, dh).transpose(0, 2, 1, 3, 4).reshape(b * r, s // r, h, dh)
        o, lse = banded_attention(to_res(q), to_res(k), to_res(v), w // r)
        outs.append(o.reshape(b, r, s // r, h, dh).transpose(0, 2, 1, 3, 4).reshape(b, s, h, dh))
        lses.append(lse.reshape(b, r, s // r, h).transpose(0, 2, 1, 3).reshape(b, s, h))
    wts = jax.nn.softmax(jnp.stack(lses, axis=-1), axis=-1)
    return jnp.einsum('bshp,pbshd->bshd', wts, jnp.stack(outs))


def nsa_attention(q, kc, vc, ks, vs, kw, vw, gate_logits, gate_b,
                  pos_k, pos_v, w1k, b1k, w2k, w1v, b1v, w2v):
    b, s, hq, dh = q.shape
    hkv = kc.shape[2]
    g = hq // hkv
    scale = dh ** -0.5
    t = jnp.arange(s)
    qg = q.reshape(b, s, hkv, g, dh)

    n_cmp = (s - NSA_CMP_L) // NSA_CMP_D + 1
    tok = jnp.arange(n_cmp)[:, None] * NSA_CMP_D + jnp.arange(NSA_CMP_L)[None, :]

    def compress(xk, pos, w1, b1, w2):
        xb = xk[:, tok] + pos[None, None, :, None, :]
        xb = xb.transpose(0, 1, 3, 2, 4).reshape(b, n_cmp, hkv, NSA_CMP_L * dh)
        return jax.nn.gelu(xb @ w1 + b1) @ w2

    k_cmp = compress(kc, pos_k, w1k, b1k, w2k)
    v_cmp = compress(vc, pos_v, w1v, b1v, w2v)
    s_c = jnp.einsum('bthgd,bihd->bhgti', qg, k_cmp, preferred_element_type=jnp.float32) * scale
    cmp_end = jnp.arange(n_cmp) * NSA_CMP_D + NSA_CMP_L - 1
    cmask = cmp_end[None, :] <= t[:, None]
    s_c = jnp.where(cmask, s_c, -jnp.inf)
    m_c = jnp.max(s_c, axis=-1, keepdims=True)
    m_c = jnp.where(jnp.isfinite(m_c), m_c, 0.0)
    e_c = jnp.exp(s_c - m_c)
    p_c = e_c / jnp.maximum(jnp.sum(e_c, axis=-1, keepdims=True), 1e-30)
    o_cmp = jnp.einsum('bhgti,bihd->bthgd', p_c, v_cmp.astype(jnp.float32))

    n_slc = s // NSA_SLC_L
    slc_start = jnp.arange(n_slc) * NSA_SLC_L
    cmp_start = jnp.arange(n_cmp) * NSA_CMP_D
    overlap = ((cmp_start[:, None] < slc_start[None, :] + NSA_SLC_L) &
               (cmp_start[:, None] + NSA_CMP_L > slc_start[None, :])).astype(jnp.float32)
    imp = jnp.einsum('bhgti,ij->bhtj', p_c, overlap)
    cur = t // NSA_SLC_L
    j = jnp.arange(n_slc)
    forced = (j[None, :] == 0) | (j[None, :] == cur[:, None]) | (j[None, :] == cur[:, None] - 1)
    causal = j[None, :] <= cur[:, None]
    score = jnp.where(forced, NSA_FORCE, jnp.where(causal, imp, -NSA_FORCE))
    k_eff = min(NSA_TOP_N, n_slc)
    _, sel = lax.top_k(score, k_eff)

    ks_b = ks.reshape(b, n_slc, NSA_SLC_L, hkv, dh).transpose(0, 3, 1, 2, 4)
    vs_b = vs.reshape(b, n_slc, NSA_SLC_L, hkv, dh).transpose(0, 3, 1, 2, 4)
    qc = min(NSA_QC, s)
    nchunk = s // qc
    q_ch = qg.reshape(b, nchunk, qc, hkv, g, dh).transpose(1, 0, 2, 3, 4, 5)
    sel_ch = sel.reshape(b, hkv, nchunk, qc, k_eff).transpose(2, 0, 1, 3, 4)
    t_ch = t.reshape(nchunk, qc)
    bi = jnp.arange(b)[:, None, None, None]
    hi = jnp.arange(hkv)[None, :, None, None]
    kk = jnp.arange(NSA_SLC_L)

    def one_chunk(args):
        qx, sx, tx = args
        kx = ks_b[bi, hi, sx]
        vx = vs_b[bi, hi, sx]
        sc = jnp.einsum('bqhgd,bhqnkd->bhgqnk', qx, kx, preferred_element_type=jnp.float32) * scale
        kpos = sx[..., None] * NSA_SLC_L + kk
        valid = (kpos <= tx[None, None, :, None, None])[:, :, None]
        sc = jnp.where(valid, sc, -jnp.inf).reshape(b, hkv, g, qc, k_eff * NSA_SLC_L)
        p = jax.nn.softmax(sc, axis=-1)
        return jnp.einsum('bhgqm,bhqmd->bqhgd', p,
                          vx.reshape(b, hkv, qc, k_eff * NSA_SLC_L, dh).astype(jnp.float32))

    o_slc = lax.map(one_chunk, (q_ch, sel_ch, t_ch))
    o_slc = o_slc.transpose(1, 0, 2, 3, 4, 5).reshape(b, s, hq, dh)

    o_win, _ = banded_attention(q, kw, vw, NSA_WIN - 1)

    gates = jax.nn.sigmoid((gate_logits + gate_b).astype(jnp.float32)).reshape(b, s, hq, 3)
    return (gates[..., 0:1] * o_cmp.reshape(b, s, hq, dh) + gates[..., 1:2] * o_slc
            + gates[..., 2:3] * o_win)


def gla_attention(q, k, v, log_a):
    b, s, h, dk = q.shape
    dv = v.shape[-1]
    C = min(GLA_CHUNK, s)
    n = s // C

    def chunks(z):
        return z.reshape(b, n, C, h, z.shape[-1]).transpose(1, 0, 3, 2, 4).astype(jnp.float32)

    qc = chunks(q) * (dk ** -0.5)
    kc, vc = chunks(k), chunks(v)
    bcum = jnp.cumsum(chunks(log_a), axis=3)
    causal = jnp.tril(jnp.ones((C, C), dtype=bool))[..., None]

    def step(state, xs):
        qx, kx, vx, bx = xs
        decay = jnp.exp(jnp.where(causal, bx[:, :, :, None, :] - bx[:, :, None, :, :], -jnp.inf))
        att = jnp.einsum('bhid,bhjd,bhijd->bhij', qx, kx, decay)
        o = (jnp.einsum('bhij,bhjv->bhiv', att, vx)
             + jnp.einsum('bhid,bhdv->bhiv', qx * jnp.exp(bx), state))
        blast = bx[:, :, -1:, :]
        state = (jnp.exp(blast[:, :, 0, :, None]) * state
                 + jnp.einsum('bhjd,bhjv->bhdv', kx * jnp.exp(blast - bx), vx))
        return state, o

    state0 = jnp.zeros((b, h, dk, dv), jnp.float32)
    _, o = lax.scan(step, state0, (qc, kc, vc, bcum))
    return o.transpose(1, 0, 3, 2, 4).reshape(b, s, h, dv)


def hybrid_layer(x, c, ln_pre, ln_post, w_ada, b_ada, w_in, w_out,
                 nsa_pos_k, nsa_pos_v, nsa_w1_k, nsa_b1_k, nsa_w2_k,
                 nsa_w1_v, nsa_b1_v, nsa_w2_v, nsa_gate_b,
                 gla_w_a2, gla_b_a, gla_norm, sinks):
    b, s, _ = x.shape
    shift, scale, gate = jnp.split(jax.nn.silu(c) @ w_ada + b_ada, 3, axis=-1)
    h = rms_norm(x, ln_pre) * (1.0 + scale[:, None]) + shift[:, None]
    u = split_columns(h @ w_in)

    def heads(z, nh):
        return z.reshape(b, s, nh, -1)

    o_a = dilated_attention(heads(u['a_q'], GROUP_HEADS), heads(u['a_k'], GROUP_HEADS),
                            heads(u['a_v'], GROUP_HEADS))
    o_b = nsa_attention(heads(u['b_q'], GROUP_HEADS),
                        heads(u['b_kc'], NSA_KV), heads(u['b_vc'], NSA_KV),
                        heads(u['b_ks'], NSA_KV), heads(u['b_vs'], NSA_KV),
                        heads(u['b_kw'], NSA_KV), heads(u['b_vw'], NSA_KV),
                        u['b_g'], nsa_gate_b, nsa_pos_k, nsa_pos_v,
                        nsa_w1_k, nsa_b1_k, nsa_w2_k, nsa_w1_v, nsa_b1_v, nsa_w2_v)
    log_a = jax.nn.log_sigmoid(u['c_a'] @ gla_w_a2 + gla_b_a) / GLA_TAU
    o_c = gla_attention(heads(u['c_q'], GROUP_HEADS), heads(u['c_k'], GROUP_HEADS),
                        heads(u['c_v'], GROUP_HEADS), heads(log_a, GROUP_HEADS))
    o_c = rms_norm(o_c, gla_norm.reshape(GROUP_HEADS, HEAD_DIM))
    o_d, _ = banded_attention(heads(u['d_q'], GROUP_HEADS), heads(u['d_k'], SWA_KV),
                              heads(u['d_v'], SWA_KV), SWA_WIN - 1, sinks)

    mixed = jnp.concatenate([
        o_a.reshape(b, s, GROUP_W).astype(x.dtype) * jax.nn.silu(u['a_z']),
        o_b.reshape(b, s, GROUP_W).astype(x.dtype) * jax.nn.silu(u['b_z']),
        o_c.reshape(b, s, GROUP_W).astype(x.dtype) * jax.nn.silu(u['c_z']),
        o_d.reshape(b, s, GROUP_W).astype(x.dtype) * jax.nn.silu(u['d_z']),
    ], axis=-1)
    y = mixed @ w_out
    return x + gate[:, None] * rms_norm(y, ln_post)


def setup_inputs(seed: int = 0) -> dict:
    key = jax.random.key(seed)
    ks = jax.random.split(key, 24)
    f32 = jnp.float32

    def nrm(k, shape, sc):
        return jax.random.normal(k, shape, f32) * sc

    cmp_in = NSA_CMP_L * HEAD_DIM
    return {
        'x': nrm(ks[0], (BATCH, SEQ, D_MODEL), 1.0),
        'c': nrm(ks[1], (BATCH, D_MODEL), 1.0),
        'ln_pre': 1.0 + nrm(ks[2], (DEPTH, D_MODEL), 0.02),
        'ln_post': 1.0 + nrm(ks[3], (DEPTH, D_MODEL), 0.02),
        'w_ada': nrm(ks[4], (DEPTH, D_MODEL, 3 * D_MODEL), 0.5 * D_MODEL ** -0.5),
        'b_ada': nrm(ks[5], (DEPTH, 3 * D_MODEL), 0.02),
        'w_in': nrm(ks[6], (DEPTH, D_MODEL, D_IN), D_MODEL ** -0.5),
        'w_out': nrm(ks[7], (DEPTH, D_MIX, D_MODEL), D_MIX ** -0.5),
        'nsa_pos_k': nrm(ks[8], (DEPTH, NSA_CMP_L, HEAD_DIM), 0.02),
        'nsa_pos_v': nrm(ks[9], (DEPTH, NSA_CMP_L, HEAD_DIM), 0.02),
        'nsa_w1_k': nrm(ks[10], (DEPTH, cmp_in, NSA_CMP_HID), cmp_in ** -0.5),
        'nsa_b1_k': nrm(ks[11], (DEPTH, NSA_CMP_HID), 0.02),
        'nsa_w2_k': nrm(ks[12], (DEPTH, NSA_CMP_HID, HEAD_DIM), NSA_CMP_HID ** -0.5),
        'nsa_w1_v': nrm(ks[13], (DEPTH, cmp_in, NSA_CMP_HID), cmp_in ** -0.5),
        'nsa_b1_v': nrm(ks[14], (DEPTH, NSA_CMP_HID), 0.02),
        'nsa_w2_v': nrm(ks[15], (DEPTH, NSA_CMP_HID, HEAD_DIM), NSA_CMP_HID ** -0.5),
        'nsa_gate_b': nrm(ks[16], (DEPTH, GROUP_HEADS * 3), 0.02),
        'gla_w_a2': nrm(ks[17], (DEPTH, GLA_RANK, GROUP_HEADS * GLA_DK), GLA_RANK ** -0.5),
        'gla_b_a': nrm(ks[18], (DEPTH, GROUP_HEADS * GLA_DK), 0.5),
        'gla_norm': 1.0 + nrm(ks[19], (DEPTH, GROUP_W), 0.02),
        'sinks': nrm(ks[20], (DEPTH, GROUP_HEADS), 0.5),
    }


def reference(x, c, ln_pre, ln_post, w_ada, b_ada, w_in, w_out,
              nsa_pos_k, nsa_pos_v, nsa_w1_k, nsa_b1_k, nsa_w2_k,
              nsa_w1_v, nsa_b1_v, nsa_w2_v, nsa_gate_b,
              gla_w_a2, gla_b_a, gla_norm, sinks):
    for i in range(DEPTH):
        x = hybrid_layer(x, c, ln_pre[i], ln_post[i], w_ada[i], b_ada[i], w_in[i], w_out[i],
                         nsa_pos_k[i], nsa_pos_v[i], nsa_w1_k[i], nsa_b1_k[i], nsa_w2_k[i],
                         nsa_w1_v[i], nsa_b1_v[i], nsa_w2_v[i], nsa_gate_b[i],
                         gla_w_a2[i], gla_b_a[i], gla_norm[i], sinks[i])
    return x
```

```python
import functools

import numpy as np
import jax
import jax.numpy as jnp
from jax import lax
from jax.experimental import pallas as pl
from jax.experimental.pallas import tpu as pltpu

F32 = jnp.float32
BF16 = jnp.bfloat16

D_MODEL = 1024
HEAD_DIM = 64
N_HEADS = 4
GROUP_W = N_HEADS * HEAD_DIM
EPS = 1e-6
NEG = -1e30
QBLK = 128

DIL_PAIRS = ((128, 1), (512, 4), (2048, 16))
NSA_CMP_L = 32
NSA_CMP_D = 16
NSA_CMP_HID = 128
NSA_SLC_L = 64
NSA_TOP_N = 16
NSA_WIN = 512
GLA_DK = 32
GLA_RANK = 16
GLA_TAU = 16.0
GLA_CHUNK = 64
SWA_WIN = 128

VMEM_LIMIT = 56 * 1024 * 1024

_ORIG_SPLITS = (
    ('a_q', 256), ('a_k', 256), ('a_v', 256), ('a_z', 256),
    ('b_q', 256), ('b_kc', 128), ('b_vc', 128), ('b_ks', 128), ('b_vs', 128),
    ('b_kw', 128), ('b_vw', 128), ('b_g', 12), ('b_z', 256),
    ('c_q', 128), ('c_k', 128), ('c_v', 256), ('c_a', 16), ('c_z', 256),
    ('d_q', 256), ('d_k', 64), ('d_v', 64), ('d_z', 256),
)
_NEW_ORDER = (
    ('z', ('a_z', 'b_z', 'c_z', 'd_z'), 1024),
    ('a_q', ('a_q',), 256), ('a_k', ('a_k',), 256), ('a_v', ('a_v',), 256),
    ('b_q', ('b_q',), 256), ('d_q', ('d_q',), 256), ('c_v', ('c_v',), 256),
    ('b_kc', ('b_kc',), 128), ('b_vc', ('b_vc',), 128), ('b_ks', ('b_ks',), 128),
    ('b_vs', ('b_vs',), 128), ('b_kw', ('b_kw',), 128), ('b_vw', ('b_vw',), 128),
    ('c_q', ('c_q',), 128), ('c_k', ('c_k',), 128), ('d_kv', ('d_k', 'd_v'), 128),
    ('misc', ('b_g', 'c_a'), 128),
)
MISC_G_OFF = 0
MISC_A_OFF = 12


def _build_layout():
    off, o = {}, 0
    for name, w in _ORIG_SPLITS:
        off[name] = (o, w)
        o += w
    src, col, n = [], {}, 0
    for name, parts, width in _NEW_ORDER:
        col[name] = n
        used = 0
        for p in parts:
            s, w = off[p]
            src.extend(range(s, s + w))
            used += w
        src.extend([-1] * (width - used))
        n += width
    return np.asarray(src, np.int32), col, n


_SRC_COLS, COL, U_W = _build_layout()


def _dot(a, b):
    return jnp.dot(a, b, preferred_element_type=F32)


def _dot_nt(a, b):
    return lax.dot_general(a, b, (((1,), (1,)), ((), ())), preferred_element_type=F32)


def _dot_tn(a, b):
    return lax.dot_general(a, b, (((0,), (0,)), ((), ())), preferred_element_type=F32)


def _split_hi_lo(x):
    hi = x.astype(BF16)
    lo = (x - hi.astype(F32)).astype(BF16)
    return hi, lo


def _dot_hl(x, w):
    hi, lo = _split_hi_lo(x)
    return _dot(hi, w) + _dot(lo, w)


def _params(n_axes):
    return pltpu.CompilerParams(dimension_semantics=("arbitrary",) * n_axes,
                                vmem_limit_bytes=VMEM_LIMIT)


def _ada_body(c_ref, w_ref, b_ref, o_ref):
    c = c_ref[...]
    sc = (c * jax.nn.sigmoid(c)).astype(BF16)
    o_ref[...] = _dot(sc, w_ref[...].astype(BF16)) + b_ref[...]


def _ada_call(c, w_ada, b_ada):
    depth, d, d3 = w_ada.shape
    b = c.shape[0]
    nb = d3 // d
    return pl.pallas_call(
        _ada_body,
        out_shape=jax.ShapeDtypeStruct((depth, b, d3), F32),
        grid=(depth, nb),
        in_specs=[
            pl.BlockSpec((b, d), lambda i, n: (0, 0)),
            pl.BlockSpec((None, d, d), lambda i, n: (i, 0, n)),
            pl.BlockSpec((None, 1, d), lambda i, n: (i, 0, n)),
        ],
        out_specs=pl.BlockSpec((None, b, d), lambda i, n: (i, 0, n)),
        compiler_params=_params(2),
        name="ada_mod",
    )(c, w_ada, b_ada.reshape(depth, 1, d3))


PROJ_TM = 512
PROJ_NCHUNK = 768


def _proj_in_body(x_ref, mod_ref, ln_ref, w_ref, o_ref):
    x = x_ref[...]
    ms = jnp.mean(x * x, axis=-1, keepdims=True)
    y = x * lax.rsqrt(ms + EPS) * ln_ref[...]
    shift = mod_ref[0:1, :]
    scale = mod_ref[1:2, :]
    h = (y * (1.0 + scale) + shift).astype(BF16)
    for n in range(0, U_W, PROJ_NCHUNK):
        o_ref[:, n:n + PROJ_NCHUNK] = _dot(h, w_ref[:, n:n + PROJ_NCHUNK]).astype(BF16)


def _proj_in_call(x2, mod3, ln_pre, w_r, seq):
    t, d = x2.shape
    tm = PROJ_TM
    per_b = seq // tm
    return pl.pallas_call(
        _proj_in_body,
        out_shape=jax.ShapeDtypeStruct((t, U_W), BF16),
        grid=(t // tm,),
        in_specs=[
            pl.BlockSpec((tm, d), lambda i: (i, 0)),
            pl.BlockSpec((None, 3, d), lambda i: (i // per_b, 0, 0)),
            pl.BlockSpec((1, d), lambda i: (0, 0)),
            pl.BlockSpec((d, U_W), lambda i: (0, 0)),
        ],
        out_specs=pl.BlockSpec((tm, U_W), lambda i: (i, 0)),
        compiler_params=_params(1),
        name="proj_in",
    )(x2, mod3, ln_pre.reshape(1, d), w_r)


def _banded_core(i, q_ref, k_src, v_src, o_ref, lse_ref, sink_ref, *, blk, span, nb, back):
    q0 = i * blk
    start = pl.multiple_of(jnp.maximum(i - nb, 0) * blk, blk)
    kb = k_src[pl.ds(start, span), :]
    vb = v_src[pl.ds(start, span), :]
    q = q_ref[...] * jnp.asarray(HEAD_DIM ** -0.5, BF16)
    row = q0 + lax.broadcasted_iota(jnp.int32, (blk, span), 0)
    col = start + lax.broadcasted_iota(jnp.int32, (blk, span), 1)
    diff = row - col
    bias = jnp.where(diff >= 0, jnp.where(diff <= back, 0.0, NEG), NEG)
    lane_h = lax.broadcasted_iota(jnp.int32, (1, GROUP_W), 1) // HEAD_DIM
    lane_l = lax.broadcasted_iota(jnp.int32, (1, 128), 1) // 32
    acc = jnp.zeros((blk, GROUP_W), F32)
    lse_acc = jnp.zeros((blk, 128), F32)
    for h in range(N_HEADS):
        hm = lane_h == h
        qh = jnp.where(hm, q, jnp.zeros_like(q))
        s = _dot_nt(qh, kb) + bias
        m = jnp.max(s, axis=-1, keepdims=True)
        if sink_ref is not None:
            sk = sink_ref[h]
            m = jnp.maximum(m, sk)
        p = jnp.exp(s - m)
        l = jnp.sum(p, axis=-1, keepdims=True)
        den = l if sink_ref is None else l + jnp.exp(sk - m)
        vh = jnp.where(hm, vb, jnp.zeros_like(vb))
        acc = acc + _dot(p.astype(BF16), vh) * (1.0 / den)
        if lse_ref is not None:
            lse_acc = jnp.where(lane_l == h, m + jnp.log(l), lse_acc)
    o_ref[...] = acc.astype(o_ref.dtype)
    if lse_ref is not None:
        lse_ref[...] = lse_acc


def _dil_body(q_ref, k_ref, v_ref, o_ref, lse_ref, **kw):
    _banded_core(pl.program_id(2), q_ref, k_ref, v_ref, o_ref, lse_ref, None, **kw)


def _dil_pair_call(u3, bsz, seq, w, r):
    ln = seq // r
    blk = min(QBLK, ln)
    back = w // r
    nb = -(-back // blk)
    span = min((nb + 1) * blk, ln)
    uv = u3.reshape(bsz, ln, r * U_W)
    cpb = U_W // GROUP_W
    cq, ck, cv = COL['a_q'] // GROUP_W, COL['a_k'] // GROUP_W, COL['a_v'] // GROUP_W
    o, lse = pl.pallas_call(
        functools.partial(_dil_body, blk=blk, span=span, nb=nb, back=back),
        out_shape=(jax.ShapeDtypeStruct((bsz, ln, r * GROUP_W), BF16),
                   jax.ShapeDtypeStruct((bsz, ln, r * 128), F32)),
        grid=(bsz, r, ln // blk),
        in_specs=[
            pl.BlockSpec((None, blk, GROUP_W), lambda b, j, i: (b, i, j * cpb + cq)),
            pl.BlockSpec((None, ln, GROUP_W), lambda b, j, i: (b, 0, j * cpb + ck)),
            pl.BlockSpec((None, ln, GROUP_W), lambda b, j, i: (b, 0, j * cpb + cv)),
        ],
        out_specs=(pl.BlockSpec((None, blk, GROUP_W), lambda b, j, i: (b, i, j)),
                   pl.BlockSpec((None, blk, 128), lambda b, j, i: (b, i, j))),
        compiler_params=_params(3),
        name=f"dil_r{r}",
    )(uv, uv, uv)
    return o.reshape(bsz * seq, GROUP_W), lse.reshape(bsz * seq, 128)


def _rep_prep(src_refs, dst_refs, seq, mode):
    ch = 512
    lane = lax.broadcasted_iota(jnp.int32, (ch, 128), 1)
    lo = lane < HEAD_DIM
    for c in range(seq // ch):
        rows = slice(c * ch, (c + 1) * ch)
        if mode == 'swa':
            x = src_refs[0][rows, :].astype(F32)
            sw = pltpu.roll(x, HEAD_DIM, 1)
            kk = jnp.where(lo, x, sw).astype(BF16)
            vv = jnp.where(lo, sw, x).astype(BF16)
            dst_refs[0][rows, 0:128] = kk
            dst_refs[0][rows, 128:256] = kk
            dst_refs[1][rows, 0:128] = vv
            dst_refs[1][rows, 128:256] = vv
        else:
            for s_ref, d_ref in zip(src_refs, dst_refs):
                x = s_ref[rows, :].astype(F32)
                sw = pltpu.roll(x, HEAD_DIM, 1)
                d_ref[rows, 0:128] = jnp.where(lo, x, sw).astype(BF16)
                d_ref[rows, 128:256] = jnp.where(lo, sw, x).astype(BF16)


def _swa_body(sink_ref, q_ref, kv_ref, o_ref, kr, vr, *, seq, **kw):
    i = pl.program_id(1)

    @pl.when(i == 0)
    def _():
        _rep_prep((kv_ref,), (kr, vr), seq, 'swa')

    _banded_core(i, q_ref, kr, vr, o_ref, None, sink_ref, **kw)


def _win_body(q_ref, kw_ref, vw_ref, o_ref, kr, vr, *, seq, **kw):
    i = pl.program_id(1)

    @pl.when(i == 0)
    def _():
        _rep_prep((kw_ref, vw_ref), (kr, vr), seq, 'win')

    _banded_core(i, q_ref, kr, vr, o_ref, None, None, **kw)


def _swa_call(u3, sinks, bsz, seq):
    blk = QBLK
    back = SWA_WIN - 1
    nb = -(-back // blk)
    span = (nb + 1) * blk
    return pl.pallas_call(
        functools.partial(_swa_body, seq=seq, blk=blk, span=span, nb=nb, back=back),
        out_shape=jax.ShapeDtypeStruct((bsz, seq, GROUP_W), BF16),
        grid=(bsz, seq // blk),
        in_specs=[
            pl.BlockSpec(memory_space=pltpu.SMEM),
            pl.BlockSpec((None, blk, GROUP_W), lambda b, i: (b, i, COL['d_q'] // GROUP_W)),
            pl.BlockSpec((None, seq, 128), lambda b, i: (b, 0, COL['d_kv'] // 128)),
        ],
        out_specs=pl.BlockSpec((None, blk, GROUP_W), lambda b, i: (b, i, 0)),
        scratch_shapes=[pltpu.VMEM((seq, GROUP_W), BF16), pltpu.VMEM((seq, GROUP_W), BF16)],
        compiler_params=_params(2),
        name="swa",
    )(sinks, u3, u3).reshape(bsz * seq, GROUP_W)


def _win_call(u3, bsz, seq):
    blk = QBLK
    back = NSA_WIN - 1
    nb = -(-back // blk)
    span = (nb + 1) * blk
    return pl.pallas_call(
        functools.partial(_win_body, seq=seq, blk=blk, span=span, nb=nb, back=back),
        out_shape=jax.ShapeDtypeStruct((bsz, seq, GROUP_W), BF16),
        grid=(bsz, seq // blk),
        in_specs=[
            pl.BlockSpec((None, blk, GROUP_W), lambda b, i: (b, i, COL['b_q'] // GROUP_W)),
            pl.BlockSpec((None, seq, 128), lambda b, i: (b, 0, COL['b_kw'] // 128)),
            pl.BlockSpec((None, seq, 128), lambda b, i: (b, 0, COL['b_vw'] // 128)),
        ],
        out_specs=pl.BlockSpec((None, blk, GROUP_W), lambda b, i: (b, i, 0)),
        scratch_shapes=[pltpu.VMEM((seq, GROUP_W), BF16), pltpu.VMEM((seq, GROUP_W), BF16)],
        compiler_params=_params(2),
        name="nsa_win",
    )(u3, u3, u3).reshape(bsz * seq, GROUP_W)


def _compress_body(xk_ref, xv_ref, wk_ref, wv_ref, pk_ref, pv_ref, w1k_ref, w1v_ref,
                   b1k_ref, b1v_ref, w2k_ref, w2v_ref, ok_ref, ov_ref, acck, accv, *, ng):
    l = pl.program_id(1)

    @pl.when(l == 0)
    def _():
        acck[...] = jnp.zeros_like(acck)
        accv[...] = jnp.zeros_like(accv)

    acck[...] += _dot(xk_ref[...], wk_ref[...])
    accv[...] += _dot(xv_ref[...], wv_ref[...])

    @pl.when(l == NSA_CMP_D - 1)
    def _():
        for acc, p_ref, w1_ref, b1_ref, w2_ref, o_ref in (
                (acck, pk_ref, w1k_ref, b1k_ref, w2k_ref, ok_ref),
                (accv, pv_ref, w1v_ref, b1v_ref, w2v_ref, ov_ref)):
            const = _dot(p_ref[...], w1_ref[...])[0:1, :] + b1_ref[...]
            out = jnp.zeros((ng, GROUP_W), F32)
            for h in range(2):
                first = acc[:, 256 * h:256 * h + 128]
                second = acc[:, 256 * h + 128:256 * h + 256]
                pre = first + pltpu.roll(second, ng - 1, 0) + const
                act = jax.nn.gelu(pre, approximate=True)
                out = out + _dot(act.astype(BF16), w2_ref[h])
            o_ref[...] = out.astype(BF16)


def _compress_call(u3, bsz, seq, wk_exp, wv_exp, posk, posv, w1k, w1v, b1k, b1v, w2k_exp, w2v_exp):
    ng = seq // NSA_CMP_D
    uv = u3.reshape(bsz, ng, NSA_CMP_D * U_W)
    cpb = U_W // 128
    ck, cv = COL['b_kc'] // 128, COL['b_vc'] // 128
    full2 = lambda shape: pl.BlockSpec(shape, lambda b, l: (0,) * len(shape))
    return pl.pallas_call(
        functools.partial(_compress_body, ng=ng),
        out_shape=(jax.ShapeDtypeStruct((bsz, ng, GROUP_W), BF16),
                   jax.ShapeDtypeStruct((bsz, ng, GROUP_W), BF16)),
        grid=(bsz, NSA_CMP_D),
        in_specs=[
            pl.BlockSpec((None, ng, 128), lambda b, l: (b, 0, l * cpb + ck)),
            pl.BlockSpec((None, ng, 128), lambda b, l: (b, 0, l * cpb + cv)),
            pl.BlockSpec((None, 128, 512), lambda b, l: (l, 0, 0)),
            pl.BlockSpec((None, 128, 512), lambda b, l: (l, 0, 0)),
            full2((8, NSA_CMP_L * HEAD_DIM)), full2((8, NSA_CMP_L * HEAD_DIM)),
            full2((NSA_CMP_L * HEAD_DIM, NSA_CMP_HID)), full2((NSA_CMP_L * HEAD_DIM, NSA_CMP_HID)),
            full2((1, NSA_CMP_HID)), full2((1, NSA_CMP_HID)),
            full2((2, NSA_CMP_HID, GROUP_W)), full2((2, NSA_CMP_HID, GROUP_W)),
        ],
        out_specs=(pl.BlockSpec((None, ng, GROUP_W), lambda b, l: (b, 0, 0)),
                   pl.BlockSpec((None, ng, GROUP_W), lambda b, l: (b, 0, 0))),
        scratch_shapes=[pltpu.VMEM((ng, 512), F32), pltpu.VMEM((ng, 512), F32)],
        compiler_params=_params(2),
        name="nsa_compress",
    )(uv, uv, wk_exp, wv_exp, posk, posv, w1k, w1v, b1k, b1v, w2k_exp, w2v_exp)


def _cmp_sel_body(q_ref, kc_ref, vc_ref, ovt_ref, eye_ref, o_ref, sb_ref, scr, *, ng, nslc):
    i = pl.program_id(1)
    q0 = i * QBLK
    q = q_ref[...] * jnp.asarray(HEAD_DIM ** -0.5, BF16)
    kc = kc_ref[...]
    vc = vc_ref[...]
    t = q0 + lax.broadcasted_iota(jnp.int32, (QBLK, ng), 0)
    ci = lax.broadcasted_iota(jnp.int32, (QBLK, ng), 1)
    valid = (ci * NSA_CMP_D + (NSA_CMP_L - 1)) <= t
    lane_h = lax.broadcasted_iota(jnp.int32, (1, GROUP_W), 1) // HEAD_DIM
    ocmp = jnp.zeros((QBLK, GROUP_W), F32)
    probs = []
    for h in range(N_HEADS):
        hm = lane_h == h
        qh = jnp.where(hm, q, jnp.zeros_like(q))
        s = jnp.where(valid, _dot_nt(qh, kc), NEG)
        m = jnp.max(s, axis=-1, keepdims=True)
        e = jnp.where(valid, jnp.exp(s - m), 0.0)
        l = jnp.sum(e, axis=-1, keepdims=True)
        p = e * (1.0 / jnp.maximum(l, 1e-30))
        ocmp = ocmp + _dot(p.astype(BF16), jnp.where(hm, vc, jnp.zeros_like(vc)))
        probs.append(p)
    o_ref[...] = ocmp.astype(BF16)

    n_free = NSA_TOP_N - 3
    nv = nslc // 8
    sub = lax.broadcasted_iota(jnp.int32, (8, QBLK), 0)
    jrow = lax.broadcasted_iota(jnp.int32, (nslc, QBLK), 0)
    cur = (q0 + lax.broadcasted_iota(jnp.int32, (nslc, QBLK), 1)) // NSA_SLC_L
    forced = jnp.where(jrow == 0, 1.0, jnp.where(jrow == cur, 1.0, jnp.where(jrow == cur - 1, 1.0, 0.0)))
    causal = jnp.where(jrow <= cur, 1.0, 0.0)
    ovt = ovt_ref[...]
    for hk in range(2):
        psum = probs[2 * hk] + probs[2 * hk + 1]
        hi, lo = _split_hi_lo(psum)
        imp_t = _dot_nt(ovt, hi) + _dot_nt(ovt, lo)
        val = jnp.where(causal * (1.0 - forced) > 0.5, imp_t, -1.0)
        scr[...] = val
        vals = [val[8 * v:8 * v + 8, :] for v in range(nv)]
        cnts = [jnp.zeros((8, QBLK), F32) for _ in range(nv)]
        for jp in range(nslc):
            rv = jnp.broadcast_to(scr[jp:jp + 1, :], (8, QBLK))
            for v in range(nv):
                if 8 * v > jp:
                    beats = jnp.where(rv >= vals[v], 1.0, 0.0)
                elif 8 * v + 7 <= jp:
                    beats = jnp.where(rv > vals[v], 1.0, 0.0)
                else:
                    tie = jnp.where(sub + 8 * v > jp, jnp.where(rv == vals[v], 1.0, 0.0), 0.0)
                    beats = jnp.where(rv > vals[v], 1.0, 0.0) + tie
                cnts[v] = cnts[v] + beats
        cnt = jnp.concatenate(cnts, axis=0)
        picked = jnp.where(cnt < n_free - 0.5, 1.0, 0.0)
        sel_t = causal * jnp.maximum(forced, picked)
        sel_t2 = jnp.concatenate([sel_t, sel_t], axis=0).astype(BF16)
        sel_q = _dot_nt(eye_ref[...], sel_t2)
        sb_ref[hk] = jnp.where(sel_q > 0.5, 0.0, NEG).astype(BF16)


def _cmp_sel_call(u3, kc_rep, vc_rep, ovt, eye, bsz, seq):
    ng = seq // NSA_CMP_D
    nslc = seq // NSA_SLC_L
    assert nslc == 64, "selection-bias layout assumes 64 selection blocks"
    return pl.pallas_call(
        functools.partial(_cmp_sel_body, ng=ng, nslc=nslc),
        out_shape=(jax.ShapeDtypeStruct((bsz, seq, GROUP_W), BF16),
                   jax.ShapeDtypeStruct((bsz, 2, seq, 128), BF16)),
        grid=(bsz, seq // QBLK),
        in_specs=[
            pl.BlockSpec((None, QBLK, GROUP_W), lambda b, i: (b, i, COL['b_q'] // GROUP_W)),
            pl.BlockSpec((None, ng, GROUP_W), lambda b, i: (b, 0, 0)),
            pl.BlockSpec((None, ng, GROUP_W), lambda b, i: (b, 0, 0)),
            pl.BlockSpec((nslc, ng), lambda b, i: (0, 0)),
            pl.BlockSpec((QBLK, QBLK), lambda b, i: (0, 0)),
        ],
        out_specs=(pl.BlockSpec((None, QBLK, GROUP_W), lambda b, i: (b, i, 0)),
                   pl.BlockSpec((None, 2, QBLK, 128), lambda b, i: (b, 0, i, 0))),
        scratch_shapes=[pltpu.VMEM((nslc, QBLK), F32)],
        compiler_params=_params(2),
        name="nsa_cmp_sel",
    )(u3, kc_rep, vc_rep, ovt, eye)


SLC_KC = 512


def _slc_body(q_ref, sb_ref, ks_ref, vs_ref, o_ref, kaug, vaug, *, seq):
    i = pl.program_id(1)
    q0 = i * QBLK

    @pl.when(i == 0)
    def _():
        ch = 512
        lane = lax.broadcasted_iota(jnp.int32, (ch, 128), 1)
        lo = lane < HEAD_DIM
        for c in range(seq // ch):
            rows = slice(c * ch, (c + 1) * ch)
            r = c * ch + lax.broadcasted_iota(jnp.int32, (ch, 128), 0)
            onehot = jnp.where((lane & (HEAD_DIM - 1)) == r // NSA_SLC_L, 1.0, 0.0)
            kf = ks_ref[rows, :].astype(F32)
            ksw = pltpu.roll(kf, HEAD_DIM, 1)
            vf = vs_ref[rows, :].astype(F32)
            vsw = pltpu.roll(vf, HEAD_DIM, 1)
            one = jnp.ones_like(vf)
            kaug[0, rows, :] = jnp.where(lo, kf, onehot).astype(BF16)
            kaug[1, rows, :] = jnp.where(lo, onehot, ksw).astype(BF16)
            kaug[2, rows, :] = jnp.where(lo, ksw, onehot).astype(BF16)
            kaug[3, rows, :] = jnp.where(lo, onehot, kf).astype(BF16)
            vaug[0, rows, :] = jnp.where(lo, vf, one).astype(BF16)
            vaug[1, rows, :] = jnp.where(lo, one, vsw).astype(BF16)
            vaug[2, rows, :] = jnp.where(lo, vsw, one).astype(BF16)
            vaug[3, rows, :] = jnp.where(lo, one, vf).astype(BF16)

    q = q_ref[...] * jnp.asarray(HEAD_DIM ** -0.5, BF16)
    lane1 = lax.broadcasted_iota(jnp.int32, (1, 128), 1)
    lo1 = lane1 < HEAD_DIM
    nch = (q0 + QBLK + SLC_KC - 1) // SLC_KC
    row = q0 + lax.broadcasted_iota(jnp.int32, (QBLK, SLC_KC), 0)
    colb = lax.broadcasted_iota(jnp.int32, (QBLK, SLC_KC), 1)
    for hk in range(2):
        qg = q[:, 128 * hk:128 * hk + 128]
        sbg = sb_ref[hk]
        res = []
        for g in range(2):
            qa = jnp.where(lo1, qg, sbg) if g == 0 else jnp.where(lo1, sbg, qg)
            idx = 2 * hk + g

            def body(c, carry, qa=qa, idx=idx):
                m, acc = carry
                k0 = pl.multiple_of(c * SLC_KC, SLC_KC)
                kb = kaug[idx, pl.ds(k0, SLC_KC), :]
                vb = vaug[idx, pl.ds(k0, SLC_KC), :]
                s = _dot_nt(qa, kb)
                s = jnp.where(colb + k0 <= row, s, NEG)
                mn = jnp.maximum(m, jnp.max(s, axis=-1, keepdims=True))
                alpha = jnp.exp(m - mn)
                p = jnp.exp(s - mn)
                acc = acc * alpha + _dot(p.astype(BF16), vb)
                return mn, acc

            m0 = jnp.full((QBLK, 1), NEG, F32)
            acc0 = jnp.zeros((QBLK, 128), F32)
            _, acc = lax.fori_loop(0, nch, body, (m0, acc0))
            is_l = (~lo1) if g == 0 else lo1
            inv = 1.0 / jnp.where(is_l, acc, 1.0)
            res.append(acc * pltpu.roll(inv, HEAD_DIM, 1))
        o_ref[:, 128 * hk:128 * hk + 128] = jnp.where(lo1, res[0], res[1]).astype(BF16)


def _slc_call(u3, sb, bsz, seq):
    return pl.pallas_call(
        functools.partial(_slc_body, seq=seq),
        out_shape=jax.ShapeDtypeStruct((bsz, seq, GROUP_W), BF16),
        grid=(bsz, seq // QBLK),
        in_specs=[
            pl.BlockSpec((None, QBLK, GROUP_W), lambda b, i: (b, i, COL['b_q'] // GROUP_W)),
            pl.BlockSpec((None, 2, QBLK, 128), lambda b, i: (b, 0, i, 0)),
            pl.BlockSpec((None, seq, 128), lambda b, i: (b, 0, COL['b_ks'] // 128)),
            pl.BlockSpec((None, seq, 128), lambda b, i: (b, 0, COL['b_vs'] // 128)),
        ],
        out_specs=pl.BlockSpec((None, QBLK, GROUP_W), lambda b, i: (b, i, 0)),
        scratch_shapes=[pltpu.VMEM((4, seq, 128), BF16), pltpu.VMEM((4, seq, 128), BF16)],
        compiler_params=_params(2),
        name="nsa_slc",
    )(u3, sb, u3, u3).reshape(bsz * seq, GROUP_W)


GLA_TC = 256
_GLA_NLEV = 6


def _gla_constants():
    c = GLA_CHUNK
    idx = np.arange(c)
    t = idx[None, :]
    mats = [(t <= idx[:, None])]
    dq, dk, masks = [], [], []
    for lev in range(_GLA_NLEV):
        m = (c // 2) >> lev
        blk = idx // m
        ref_q = blk * m
        ref_k = (blk + 1) * m
        odd = (blk % 2 == 1)
        dq.append(((t > ref_q[:, None]) & (t <= idx[:, None]) & odd[:, None]))
        dk.append(((t > idx[:, None]) & (t <= ref_k[:, None]) & (~odd)[:, None]))
        same_parent = (idx[:, None] // (2 * m)) == (idx[None, :] // (2 * m))
        masks.append(same_parent & odd[:, None] & (~odd)[None, :])
    masks.append(np.eye(c, dtype=bool))
    tail = (t > idx[:, None])
    dall = np.concatenate(mats + dq + dk + [tail], axis=0).astype(np.float32)
    mask4 = np.stack([np.tile(mk.astype(np.float32), (N_HEADS, 1)) for mk in masks])
    hv = np.arange(GROUP_W) // HEAD_DIM
    hd = np.arange(N_HEADS * GLA_DK) // GLA_DK
    bd = (hv[:, None] == hd[None, :]).astype(np.float32)
    same_head = (hv[:, None] == hv[None, :]).astype(np.float32)
    return dall, mask4, bd, same_head


def _gla_body(cq_ref, ck_ref, cv_ref, misc_ref, wa_ref, ba_ref, dall_ref, mask_ref, bd_ref,
              sh_ref, gn_ref, o_ref, st):
    j = pl.program_id(1)

    @pl.when(j == 0)
    def _():
        st[...] = jnp.zeros_like(st)

    c = GLA_CHUNK
    nk = N_HEADS * GLA_DK
    lane_h = lax.broadcasted_iota(jnp.int32, (1, nk), 1) // GLA_DK
    lane_v = lax.broadcasted_iota(jnp.int32, (1, GROUP_W), 1) // HEAD_DIM
    dall = dall_ref[...]
    for ch in range(GLA_TC // c):
        rows = slice(ch * c, (ch + 1) * c)
        z = _dot(misc_ref[rows, :], wa_ref[...]) + ba_ref[...]
        la = (jnp.minimum(z, 0.0) - jnp.log1p(jnp.exp(-jnp.abs(z)))) * (1.0 / GLA_TAU)
        hi, lo = _split_hi_lo(la)
        e_all = _dot(dall, hi) + _dot(dall, lo)
        bcum = e_all[0:c]
        qf = cq_ref[rows, :].astype(F32) * (GLA_DK ** -0.5)
        kf = ck_ref[rows, :].astype(F32)
        v = cv_ref[rows, :]
        att = jnp.zeros((N_HEADS * c, c), F32)
        for lev in range(_GLA_NLEV + 1):
            if lev < _GLA_NLEV:
                ql = qf * jnp.exp(e_all[(1 + lev) * c:(2 + lev) * c])
                kl = kf * jnp.exp(e_all[(1 + _GLA_NLEV + lev) * c:(2 + _GLA_NLEV + lev) * c])
            else:
                ql, kl = qf, kf
            qs = jnp.concatenate([jnp.where(lane_h == h, ql, 0.0) for h in range(N_HEADS)], axis=0)
            att = att + _dot_nt(qs.astype(BF16), kl.astype(BF16)) * mask_ref[lev]
        attb = att.astype(BF16)
        st_old = st[...]
        o = _dot_nt((qf * jnp.exp(bcum)).astype(BF16), st_old.astype(BF16))
        for h in range(N_HEADS):
            vh = jnp.where(lane_v == h, v, jnp.zeros_like(v))
            o = o + _dot(attb[h * c:(h + 1) * c, :], vh)
        kt = (kf * jnp.exp(e_all[(1 + 2 * _GLA_NLEV) * c:(2 + 2 * _GLA_NLEV) * c])).astype(BF16)
        st[...] = st_old * jnp.exp(bcum[c - 1:c, :]) + bd_ref[...] * _dot_tn(v, kt)
        ms = _dot_hl(o * o, sh_ref[...]) * (1.0 / HEAD_DIM)
        o_ref[rows, :] = (o * lax.rsqrt(ms + EPS) * gn_ref[...]).astype(BF16)


def _gla_call(u3, wa_emb, ba, gnorm, bsz, seq):
    dall, mask4, bd, same_head = _gla_constants()
    tc = GLA_TC
    full = lambda shape: pl.BlockSpec(shape, lambda b, j: (0,) * len(shape))
    return pl.pallas_call(
        _gla_body,
        out_shape=jax.ShapeDtypeStruct((bsz, seq, GROUP_W), BF16),
        grid=(bsz, seq // tc),
        in_specs=[
            pl.BlockSpec((None, tc, 128), lambda b, j: (b, j, COL['c_q'] // 128)),
            pl.BlockSpec((None, tc, 128), lambda b, j: (b, j, COL['c_k'] // 128)),
            pl.BlockSpec((None, tc, GROUP_W), lambda b, j: (b, j, COL['c_v'] // GROUP_W)),
            pl.BlockSpec((None, tc, 128), lambda b, j: (b, j, COL['misc'] // 128)),
            full((128, 128)), full((1, 128)), full(dall.shape), full(mask4.shape),
            full(bd.shape), full(same_head.shape), full((1, GROUP_W)),
        ],
        out_specs=pl.BlockSpec((None, tc, GROUP_W), lambda b, j: (b, j, 0)),
        scratch_shapes=[pltpu.VMEM((GROUP_W, N_HEADS * GLA_DK), F32)],
        compiler_params=_params(2),
        name="gla",
    )(u3, u3, u3, u3, wa_emb, ba, jnp.asarray(dall, BF16), jnp.asarray(mask4, F32),
      jnp.asarray(bd, F32), jnp.asarray(same_head, BF16), gnorm).reshape(bsz * seq, GROUP_W)


OUT_TM = 512


def _out_body(x_ref, mod_ref, ln_ref, z_ref, o1_ref, o2_ref, o3_ref, l1_ref, l2_ref, l3_ref,
              ocmp_ref, oslc_ref, owin_ref, oc_ref, od_ref, misc_ref, gb_ref, e4_ref, eg_ref,
              w_ref, out_ref):
    l1, l2, l3 = l1_ref[...], l2_ref[...], l3_ref[...]
    mx = jnp.maximum(jnp.maximum(l1, l2), l3)
    e1, e2, e3 = jnp.exp(l1 - mx), jnp.exp(l2 - mx), jnp.exp(l3 - mx)
    inv = 1.0 / (e1 + e2 + e3)
    e4 = e4_ref[...]
    oa = jnp.zeros((OUT_TM, GROUP_W), F32)
    for e, o_ref in ((e1, o1_ref), (e2, o2_ref), (e3, o3_ref)):
        oa = oa + _dot_hl(e * inv, e4) * o_ref[...].astype(F32)
    sg = jax.nn.sigmoid(misc_ref[...].astype(F32) + gb_ref[...])
    sg_hi, sg_lo = _split_hi_lo(sg)
    ob = jnp.zeros((OUT_TM, GROUP_W), F32)
    for br, o_ref in enumerate((ocmp_ref, oslc_ref, owin_ref)):
        g = _dot(sg_hi, eg_ref[br]) + _dot(sg_lo, eg_ref[br])
        ob = ob + g * o_ref[...].astype(F32)
    z = z_ref[...].astype(F32)
    sz = z * jax.nn.sigmoid(z)
    mixed = jnp.concatenate([oa, ob, oc_ref[...].astype(F32), od_ref[...].astype(F32)], axis=1) * sz
    y = _dot(mixed.astype(BF16), w_ref[...])
    ms = jnp.mean(y * y, axis=-1, keepdims=True)
    yn = y * lax.rsqrt(ms + EPS) * ln_ref[...]
    out_ref[...] = x_ref[...] + mod_ref[2:3, :] * yn


def _out_call(x2, mod3, ln_post, u2, o1, o2, o3, l1, l2, l3, ocmp, oslc, owin, oc, od,
              gate_b_row, e4, eg, w_out, seq):
    t, d = x2.shape
    tm = OUT_TM
    per_b = seq // tm
    row = lambda w: pl.BlockSpec((tm, w), lambda i: (i, 0))
    full = lambda shape: pl.BlockSpec(shape, lambda i: (0,) * len(shape))
    return pl.pallas_call(
        _out_body,
        out_shape=jax.ShapeDtypeStruct((t, d), F32),
        grid=(t // tm,),
        in_specs=[
            row(d),
            pl.BlockSpec((None, 3, d), lambda i: (i // per_b, 0, 0)),
            full((1, d)),
            pl.BlockSpec((tm, 1024), lambda i: (i, COL['z'] // 1024)),
            row(GROUP_W), row(GROUP_W), row(GROUP_W), row(128), row(128), row(128),
            row(GROUP_W), row(GROUP_W), row(GROUP_W), row(GROUP_W), row(GROUP_W),
            pl.BlockSpec((tm, 128), lambda i: (i, COL['misc'] // 128)),
            full((1, 128)), full((128, GROUP_W)), full((3, 128, GROUP_W)), full((d, d)),
        ],
        out_specs=row(d),
        compiler_params=_params(1),
        name="mix_out",
    )(x2, mod3, ln_post.reshape(1, d), u2, o1, o2, o3, l1, l2, l3, ocmp, oslc, owin, oc, od,
      u2, gate_b_row, e4, eg, w_out)


def _expand_w1(w1):
    w = w1.reshape(2, NSA_CMP_D, HEAD_DIM, NSA_CMP_HID)
    zero = jnp.zeros((NSA_CMP_D, HEAD_DIM, NSA_CMP_HID), w1.dtype)
    top = jnp.concatenate([w[0], w[1], zero, zero], axis=-1)
    bot = jnp.concatenate([zero, zero, w[0], w[1]], axis=-1)
    return jnp.concatenate([top, bot], axis=1).astype(BF16)


def _expand_w2(w2):
    z = jnp.zeros_like(w2)
    h0 = jnp.concatenate([w2, w2, z, z], axis=1)
    h1 = jnp.concatenate([z, z, w2, w2], axis=1)
    return jnp.stack([h0, h1]).astype(BF16)


def _static_tables(seq):
    ng = seq // NSA_CMP_D
    nslc = seq // NSA_SLC_L
    cs = np.arange(ng) * NSA_CMP_D
    ss = np.arange(nslc) * NSA_SLC_L
    ovt = ((cs[None, :] < ss[:, None] + NSA_SLC_L) & (cs[None, :] + NSA_CMP_L > ss[:, None]))
    ovt[:, ng - 1] = False
    e4 = np.zeros((128, GROUP_W), np.float32)
    eg = np.zeros((3, 128, GROUP_W), np.float32)
    for h in range(N_HEADS):
        e4[32 * h, HEAD_DIM * h:HEAD_DIM * (h + 1)] = 1.0
        for br in range(3):
            eg[br, MISC_G_OFF + 3 * h + br, HEAD_DIM * h:HEAD_DIM * (h + 1)] = 1.0
    return (jnp.asarray(ovt, BF16), jnp.asarray(np.eye(QBLK), BF16),
            jnp.asarray(e4, BF16), jnp.asarray(eg, BF16))


def kernel(x, c, ln_pre, ln_post, w_ada, b_ada, w_in, w_out, nsa_pos_k, nsa_pos_v, nsa_w1_k, nsa_b1_k, nsa_w2_k, nsa_w1_v, nsa_b1_v, nsa_w2_v, nsa_gate_b, gla_w_a2, gla_b_a, gla_norm, sinks):
    bsz, seq, d = x.shape
    depth = w_in.shape[0]
    t = bsz * seq

    src = jnp.asarray(np.maximum(_SRC_COLS, 0))
    keep = jnp.asarray((_SRC_COLS >= 0).astype(np.float32))
    w_in_r = (jnp.take(w_in, src, axis=2) * keep).astype(BF16)
    w_out_b = w_out.astype(BF16)
    ovt, eye, e4, eg = _static_tables(seq)

    mod = _ada_call(c, w_ada, b_ada).reshape(depth, bsz, 3, d)

    x2 = x.reshape(t, d)
    for i in range(depth):
        u2 = _proj_in_call(x2, mod[i], ln_pre[i], w_in_r[i], seq)
        u3 = u2.reshape(bsz, seq, U_W)

        dil = [_dil_pair_call(u3, bsz, seq, w, r) for w, r in DIL_PAIRS]

        kc_rep, vc_rep = _compress_call(
            u3, bsz, seq, _expand_w1(nsa_w1_k[i]), _expand_w1(nsa_w1_v[i]),
            jnp.broadcast_to(nsa_pos_k[i].reshape(1, -1), (8, NSA_CMP_L * HEAD_DIM)).astype(BF16),
            jnp.broadcast_to(nsa_pos_v[i].reshape(1, -1), (8, NSA_CMP_L * HEAD_DIM)).astype(BF16),
            nsa_w1_k[i].astype(BF16), nsa_w1_v[i].astype(BF16),
            nsa_b1_k[i].reshape(1, -1), nsa_b1_v[i].reshape(1, -1),
            _expand_w2(nsa_w2_k[i]), _expand_w2(nsa_w2_v[i]))
        ocmp, sb = _cmp_sel_call(u3, kc_rep, vc_rep, ovt, eye, bsz, seq)
        oslc = _slc_call(u3, sb, bsz, seq)
        owin = _win_call(u3, bsz, seq)

        wa_emb = jnp.zeros((128, 128), F32).at[MISC_A_OFF:MISC_A_OFF + GLA_RANK, :].set(gla_w_a2[i]).astype(BF16)
        oc = _gla_call(u3, wa_emb, gla_b_a[i].reshape(1, -1), gla_norm[i].reshape(1, -1), bsz, seq)

        od = _swa_call(u3, sinks[i], bsz, seq)

        gate_b_row = jnp.zeros((1, 128), F32).at[0, MISC_G_OFF:MISC_G_OFF + 3 * N_HEADS].set(nsa_gate_b[i])
        x2 = _out_call(x2, mod[i], ln_post[i], u2, dil[0][0], dil[1][0], dil[2][0],
                       dil[0][1], dil[1][1], dil[2][1], ocmp.reshape(t, GROUP_W), oslc, owin, oc, od,
                       gate_b_row, e4, eg, w_out_b[i], seq)
    return x2.reshape(bsz, seq, d)
```

```python
import functools

import numpy as np
import jax
import jax.numpy as jnp
from jax import lax
from jax.experimental import pallas as pl
from jax.experimental.pallas import tpu as pltpu

F32 = jnp.float32
BF16 = jnp.bfloat16

D_MODEL = 1024
HEAD_DIM = 64
N_HEADS = 4
GROUP_W = N_HEADS * HEAD_DIM
EPS = 1e-6
NEG = -1e30
QBLK = 128

DIL_PAIRS = ((128, 1), (512, 4), (2048, 16))
NSA_CMP_L = 32
NSA_CMP_D = 16
NSA_CMP_HID = 128
NSA_SLC_L = 64
NSA_TOP_N = 16
NSA_WIN = 512
GLA_DK = 32
GLA_RANK = 16
GLA_TAU = 16.0
GLA_CHUNK = 64
SWA_WIN = 128

VMEM_LIMIT = 56 * 1024 * 1024

_ORIG_SPLITS = (
    ('a_q', 256), ('a_k', 256), ('a_v', 256), ('a_z', 256),
    ('b_q', 256), ('b_kc', 128), ('b_vc', 128), ('b_ks', 128), ('b_vs', 128),
    ('b_kw', 128), ('b_vw', 128), ('b_g', 12), ('b_z', 256),
    ('c_q', 128), ('c_k', 128), ('c_v', 256), ('c_a', 16), ('c_z', 256),
    ('d_q', 256), ('d_k', 64), ('d_v', 64), ('d_z', 256),
)
_NEW_ORDER = (
    ('a_q', ('a_q',), 256), ('a_k', ('a_k',), 256), ('a_v', ('a_v',), 256),
    ('b_kc', ('b_kc',), 128), ('b_vc', ('b_vc',), 128),
    ('z', ('a_z', 'b_z', 'c_z', 'd_z'), 1024),
    ('b_q', ('b_q',), 256), ('d_q', ('d_q',), 256), ('c_v', ('c_v',), 256),
    ('b_ks', ('b_ks',), 128), ('b_vs', ('b_vs',), 128), ('b_kw', ('b_kw',), 128),
    ('b_vw', ('b_vw',), 128), ('c_q', ('c_q',), 128), ('c_k', ('c_k',), 128),
    ('d_kv', ('d_k', 'd_v'), 128), ('misc', ('b_g', 'c_a'), 128),
)
A_W = 768
C_W = 256
MISC_G_OFF = 0
MISC_A_OFF = 12


def _build_layout():
    off, o = {}, 0
    for name, w in _ORIG_SPLITS:
        off[name] = (o, w)
        o += w
    src, col, n = [], {}, 0
    for name, parts, width in _NEW_ORDER:
        col[name] = n
        used = 0
        for p in parts:
            s, w = off[p]
            src.extend(range(s, s + w))
            used += w
        src.extend([-1] * (width - used))
        n += width
    return np.asarray(src, np.int32), col, n


_SRC_COLS, COL, U_W = _build_layout()
assert COL['a_q'] == 0 and COL['b_kc'] == A_W and COL['z'] == A_W + C_W


def _dot(a, b):
    return jnp.dot(a, b, preferred_element_type=F32)


def _dot_nt(a, b):
    return lax.dot_general(a, b, (((1,), (1,)), ((), ())), preferred_element_type=F32)


def _dot_tn(a, b):
    return lax.dot_general(a, b, (((0,), (0,)), ((), ())), preferred_element_type=F32)


def _split_hi_lo(x):
    hi = x.astype(BF16)
    lo = (x - hi.astype(F32)).astype(BF16)
    return hi, lo


def _dot_hl(x, w):
    hi, lo = _split_hi_lo(x)
    return _dot(hi, w) + _dot(lo, w)


def _dot_hl2(x, w2):
    hi, lo = _split_hi_lo(x)
    return _dot(jnp.concatenate([hi, lo], axis=1), w2)


def _params(n_axes):
    return pltpu.CompilerParams(dimension_semantics=("arbitrary",) * n_axes,
                                vmem_limit_bytes=VMEM_LIMIT)


def _ada_body(c_ref, w_ref, b_ref, o_ref):
    c = c_ref[...]
    sc = (c * jax.nn.sigmoid(c)).astype(BF16)
    o_ref[...] = _dot(sc, w_ref[...].astype(BF16)) + b_ref[...]


def _ada_call(c, w_ada, b_ada):
    depth, d, d3 = w_ada.shape
    b = c.shape[0]
    nb = d3 // d
    return pl.pallas_call(
        _ada_body,
        out_shape=jax.ShapeDtypeStruct((depth, b, d3), F32),
        grid=(depth, nb),
        in_specs=[
            pl.BlockSpec((b, d), lambda i, n: (0, 0)),
            pl.BlockSpec((None, d, d), lambda i, n: (i, 0, n)),
            pl.BlockSpec((None, 1, d), lambda i, n: (i, 0, n)),
        ],
        out_specs=pl.BlockSpec((None, b, d), lambda i, n: (i, 0, n)),
        compiler_params=_params(2),
        name="ada_mod",
    )(c, w_ada, b_ada.reshape(depth, 1, d3))


PROJ_TM = 512
_PROJ_CHUNKS = ((0, 768), (768, 1024), (1024, 1792), (1792, 2560), (2560, 3328), (3328, 3840))


def _proj_in_body(x_ref, mod_ref, ln_ref, w_ref, o_ref, oa_ref, oc_ref):
    x = x_ref[...]
    ms = jnp.mean(x * x, axis=-1, keepdims=True)
    y = x * lax.rsqrt(ms + EPS) * ln_ref[...]
    shift = mod_ref[0:1, :]
    scale = mod_ref[1:2, :]
    h = (y * (1.0 + scale) + shift).astype(BF16)
    for n0, n1 in _PROJ_CHUNKS:
        r = _dot(h, w_ref[:, n0:n1]).astype(BF16)
        o_ref[:, n0:n1] = r
        if (n0, n1) == (0, A_W):
            oa_ref[...] = r
        elif (n0, n1) == (A_W, A_W + C_W):
            oc_ref[...] = r


def _proj_in_call(x2, mod3, ln_pre, w_r, seq):
    t, d = x2.shape
    tm = PROJ_TM
    per_b = seq // tm
    return pl.pallas_call(
        _proj_in_body,
        out_shape=(jax.ShapeDtypeStruct((t, U_W), BF16),
                   jax.ShapeDtypeStruct((t, A_W), BF16),
                   jax.ShapeDtypeStruct((t, C_W), BF16)),
        grid=(t // tm,),
        in_specs=[
            pl.BlockSpec((tm, d), lambda i: (i, 0)),
            pl.BlockSpec((None, 3, d), lambda i: (i // per_b, 0, 0)),
            pl.BlockSpec((1, d), lambda i: (0, 0)),
            pl.BlockSpec((d, U_W), lambda i: (0, 0)),
        ],
        out_specs=(pl.BlockSpec((tm, U_W), lambda i: (i, 0)),
                   pl.BlockSpec((tm, A_W), lambda i: (i, 0)),
                   pl.BlockSpec((tm, C_W), lambda i: (i, 0))),
        compiler_params=_params(1),
        name="proj_in",
    )(x2, mod3, ln_pre.reshape(1, d), w_r)


def _stack_heads(q):
    lane_h = lax.broadcasted_iota(jnp.int32, (1, GROUP_W), 1) // HEAD_DIM
    zero = jnp.zeros_like(q)
    return jnp.concatenate([jnp.where(lane_h == h, q, zero) for h in range(N_HEADS)], axis=0)


def _pick_heads(r):
    lane_h = lax.broadcasted_iota(jnp.int32, (1, GROUP_W), 1) // HEAD_DIM
    out = jnp.where(lane_h == 0, r[0], 0.0)
    for h in range(1, N_HEADS):
        out = jnp.where(lane_h == h, r[h], out)
    return out


def _banded_core(i, q_ref, k_src, v_src, o_ref, lse_ref, sink_ref, *, blk, nsub, span, nb, back):
    if sink_ref is not None:
        hidx = lax.broadcasted_iota(jnp.int32, (N_HEADS, 1, 1), 0)
        sink = jnp.zeros((N_HEADS, 1, 1), F32)
        for h in range(N_HEADS):
            sink = jnp.where(hidx == h, sink_ref[h], sink)
    lane_l = lax.broadcasted_iota(jnp.int32, (1, 128), 1) // 32
    for sb in range(nsub):
        ib = i * nsub + sb
        q0 = ib * blk
        start = pl.multiple_of(jnp.maximum(ib - nb, 0) * blk, blk)
        kb = k_src[pl.ds(start, span), :]
        vb = v_src[pl.ds(start, span), :]
        q = q_ref[sb * blk:(sb + 1) * blk, :] * jnp.asarray(HEAD_DIM ** -0.5, BF16)
        row = q0 + lax.broadcasted_iota(jnp.int32, (blk, span), 0)
        col = start + lax.broadcasted_iota(jnp.int32, (blk, span), 1)
        diff = row - col
        bias = jnp.where(diff >= 0, jnp.where(diff <= back, 0.0, NEG), NEG)
        s = _dot_nt(_stack_heads(q), kb).reshape(N_HEADS, blk, span) + bias[None]
        m = jnp.max(s, axis=-1, keepdims=True)
        if sink_ref is not None:
            m = jnp.maximum(m, sink)
        p = jnp.exp(s - m)
        l = jnp.sum(p, axis=-1, keepdims=True)
        den = l if sink_ref is None else l + jnp.exp(sink - m)
        r = _dot(p.reshape(N_HEADS * blk, span).astype(BF16), vb).reshape(N_HEADS, blk, GROUP_W)
        o_ref[sb * blk:(sb + 1) * blk, :] = _pick_heads(r * (1.0 / den)).astype(o_ref.dtype)
        if lse_ref is not None:
            lse = m + jnp.log(l)
            acc = jnp.zeros((blk, 128), F32)
            for h in range(N_HEADS):
                acc = jnp.where(lane_l == h, lse[h], acc)
            lse_ref[sb * blk:(sb + 1) * blk, :] = acc


def _dil_body(q_ref, k_ref, v_ref, o_ref, lse_ref, **kw):
    _banded_core(pl.program_id(2), q_ref, k_ref, v_ref, o_ref, lse_ref, None, **kw)


def _banded_geometry(ln, back):
    blk = min(QBLK, ln)
    nb = -(-back // blk)
    span = min((nb + 1) * blk, ln)
    nsub = 2 if ln % (2 * blk) == 0 else 1
    return blk, nb, span, nsub


def _dil_pair_call(ua, bsz, seq, w, r):
    ln = seq // r
    back = w // r
    blk, nb, span, nsub = _banded_geometry(ln, back)
    rows = blk * nsub
    uv = ua.reshape(bsz, ln, r * A_W)
    cpb = A_W // GROUP_W
    o, lse = pl.pallas_call(
        functools.partial(_dil_body, blk=blk, nsub=nsub, span=span, nb=nb, back=back),
        out_shape=(jax.ShapeDtypeStruct((bsz, ln, r * GROUP_W), BF16),
                   jax.ShapeDtypeStruct((bsz, ln, r * 128), F32)),
        grid=(bsz, r, ln // rows),
        in_specs=[
            pl.BlockSpec((None, rows, GROUP_W), lambda b, j, i: (b, i, j * cpb)),
            pl.BlockSpec((None, ln, GROUP_W), lambda b, j, i: (b, 0, j * cpb + 1)),
            pl.BlockSpec((None, ln, GROUP_W), lambda b, j, i: (b, 0, j * cpb + 2)),
        ],
        out_specs=(pl.BlockSpec((None, rows, GROUP_W), lambda b, j, i: (b, i, j)),
                   pl.BlockSpec((None, rows, 128), lambda b, j, i: (b, i, j))),
        compiler_params=_params(3),
        name=f"dil_r{r}",
    )(uv, uv, uv)
    return o.reshape(bsz * seq, GROUP_W), lse.reshape(bsz * seq, 128)


def _rep_prep(src_refs, dst_refs, seq, mode):
    ch = 512
    lane = lax.broadcasted_iota(jnp.int32, (ch, 128), 1)
    lo = lane < HEAD_DIM
    for c in range(seq // ch):
        rows = slice(c * ch, (c + 1) * ch)
        if mode == 'swa':
            x = src_refs[0][rows, :].astype(F32)
            sw = pltpu.roll(x, HEAD_DIM, 1)
            kk = jnp.where(lo, x, sw).astype(BF16)
            vv = jnp.where(lo, sw, x).astype(BF16)
            dst_refs[0][rows, 0:128] = kk
            dst_refs[0][rows, 128:256] = kk
            dst_refs[1][rows, 0:128] = vv
            dst_refs[1][rows, 128:256] = vv
        else:
            for s_ref, d_ref in zip(src_refs, dst_refs):
                x = s_ref[rows, :].astype(F32)
                sw = pltpu.roll(x, HEAD_DIM, 1)
                d_ref[rows, 0:128] = jnp.where(lo, x, sw).astype(BF16)
                d_ref[rows, 128:256] = jnp.where(lo, sw, x).astype(BF16)


def _swa_body(sink_ref, q_ref, kv_ref, o_ref, kr, vr, *, seq, **kw):
    i = pl.program_id(1)

    @pl.when(i == 0)
    def _():
        _rep_prep((kv_ref,), (kr, vr), seq, 'swa')

    _banded_core(i, q_ref, kr, vr, o_ref, None, sink_ref, **kw)


def _win_body(q_ref, kw_ref, vw_ref, o_ref, kr, vr, *, seq, **kw):
    i = pl.program_id(1)

    @pl.when(i == 0)
    def _():
        _rep_prep((kw_ref, vw_ref), (kr, vr), seq, 'win')

    _banded_core(i, q_ref, kr, vr, o_ref, None, None, **kw)


def _swa_call(u3, sinks, bsz, seq):
    back = SWA_WIN - 1
    blk, nb, span, nsub = _banded_geometry(seq, back)
    rows = blk * nsub
    return pl.pallas_call(
        functools.partial(_swa_body, seq=seq, blk=blk, nsub=nsub, span=span, nb=nb, back=back),
        out_shape=jax.ShapeDtypeStruct((bsz, seq, GROUP_W), BF16),
        grid=(bsz, seq // rows),
        in_specs=[
            pl.BlockSpec(memory_space=pltpu.SMEM),
            pl.BlockSpec((None, rows, GROUP_W), lambda b, i: (b, i, COL['d_q'] // GROUP_W)),
            pl.BlockSpec((None, seq, 128), lambda b, i: (b, 0, COL['d_kv'] // 128)),
        ],
        out_specs=pl.BlockSpec((None, rows, GROUP_W), lambda b, i: (b, i, 0)),
        scratch_shapes=[pltpu.VMEM((seq, GROUP_W), BF16), pltpu.VMEM((seq, GROUP_W), BF16)],
        compiler_params=_params(2),
        name="swa",
    )(sinks, u3, u3).reshape(bsz * seq, GROUP_W)


def _win_call(u3, bsz, seq):
    back = NSA_WIN - 1
    blk, nb, span, nsub = _banded_geometry(seq, back)
    rows = blk * nsub
    return pl.pallas_call(
        functools.partial(_win_body, seq=seq, blk=blk, nsub=nsub, span=span, nb=nb, back=back),
        out_shape=jax.ShapeDtypeStruct((bsz, seq, GROUP_W), BF16),
        grid=(bsz, seq // rows),
        in_specs=[
            pl.BlockSpec((None, rows, GROUP_W), lambda b, i: (b, i, COL['b_q'] // GROUP_W)),
            pl.BlockSpec((None, seq, 128), lambda b, i: (b, 0, COL['b_kw'] // 128)),
            pl.BlockSpec((None, seq, 128), lambda b, i: (b, 0, COL['b_vw'] // 128)),
        ],
        out_specs=pl.BlockSpec((None, rows, GROUP_W), lambda b, i: (b, i, 0)),
        scratch_shapes=[pltpu.VMEM((seq, GROUP_W), BF16), pltpu.VMEM((seq, GROUP_W), BF16)],
        compiler_params=_params(2),
        name="nsa_win",
    )(u3, u3, u3).reshape(bsz * seq, GROUP_W)


def _compress_body(xk_ref, xv_ref, wk_ref, wv_ref, pk_ref, pv_ref, w1k_ref, w1v_ref,
                   b1k_ref, b1v_ref, w2k_ref, w2v_ref, ok_ref, ov_ref, acck, accv, *, ng):
    l = pl.program_id(1)

    @pl.when(l == 0)
    def _():
        acck[...] = jnp.zeros_like(acck)
        accv[...] = jnp.zeros_like(accv)

    acck[...] += _dot(xk_ref[...], wk_ref[...])
    accv[...] += _dot(xv_ref[...], wv_ref[...])

    @pl.when(l == NSA_CMP_D - 1)
    def _():
        for acc, p_ref, w1_ref, b1_ref, w2_ref, o_ref in (
                (acck, pk_ref, w1k_ref, b1k_ref, w2k_ref, ok_ref),
                (accv, pv_ref, w1v_ref, b1v_ref, w2v_ref, ov_ref)):
            const = _dot(p_ref[...], w1_ref[...])[0:1, :] + b1_ref[...]
            out = jnp.zeros((ng, GROUP_W), F32)
            for h in range(2):
                first = acc[:, 256 * h:256 * h + 128]
                second = acc[:, 256 * h + 128:256 * h + 256]
                pre = first + pltpu.roll(second, ng - 1, 0) + const
                act = jax.nn.gelu(pre, approximate=True)
                out = out + _dot(act.astype(BF16), w2_ref[h])
            o_ref[...] = out.astype(BF16)


def _compress_call(uc, bsz, seq, wk_exp, wv_exp, posk, posv, w1k, w1v, b1k, b1v, w2k_exp, w2v_exp):
    ng = seq // NSA_CMP_D
    uv = uc.reshape(bsz, ng, NSA_CMP_D * C_W)
    cpb = C_W // 128
    full2 = lambda shape: pl.BlockSpec(shape, lambda b, l: (0,) * len(shape))
    return pl.pallas_call(
        functools.partial(_compress_body, ng=ng),
        out_shape=(jax.ShapeDtypeStruct((bsz, ng, GROUP_W), BF16),
                   jax.ShapeDtypeStruct((bsz, ng, GROUP_W), BF16)),
        grid=(bsz, NSA_CMP_D),
        in_specs=[
            pl.BlockSpec((None, ng, 128), lambda b, l: (b, 0, l * cpb)),
            pl.BlockSpec((None, ng, 128), lambda b, l: (b, 0, l * cpb + 1)),
            pl.BlockSpec((None, 128, 512), lambda b, l: (l, 0, 0)),
            pl.BlockSpec((None, 128, 512), lambda b, l: (l, 0, 0)),
            full2((8, NSA_CMP_L * HEAD_DIM)), full2((8, NSA_CMP_L * HEAD_DIM)),
            full2((NSA_CMP_L * HEAD_DIM, NSA_CMP_HID)), full2((NSA_CMP_L * HEAD_DIM, NSA_CMP_HID)),
            full2((1, NSA_CMP_HID)), full2((1, NSA_CMP_HID)),
            full2((2, NSA_CMP_HID, GROUP_W)), full2((2, NSA_CMP_HID, GROUP_W)),
        ],
        out_specs=(pl.BlockSpec((None, ng, GROUP_W), lambda b, l: (b, 0, 0)),
                   pl.BlockSpec((None, ng, GROUP_W), lambda b, l: (b, 0, 0))),
        scratch_shapes=[pltpu.VMEM((ng, 512), F32), pltpu.VMEM((ng, 512), F32)],
        compiler_params=_params(2),
        name="nsa_compress",
    )(uv, uv, wk_exp, wv_exp, posk, posv, w1k, w1v, b1k, b1v, w2k_exp, w2v_exp)


def _cmp_sel_body(q_ref, kc_ref, vc_ref, ovt_ref, eye_ref, o_ref, sb_ref, scr, *, ng, nslc):
    i = pl.program_id(1)
    q0 = i * QBLK
    q = q_ref[...] * jnp.asarray(HEAD_DIM ** -0.5, BF16)
    t = q0 + lax.broadcasted_iota(jnp.int32, (QBLK, ng), 0)
    ci = lax.broadcasted_iota(jnp.int32, (QBLK, ng), 1)
    valid = ((ci * NSA_CMP_D + (NSA_CMP_L - 1)) <= t)[None]
    s = jnp.where(valid, _dot_nt(_stack_heads(q), kc_ref[...]).reshape(N_HEADS, QBLK, ng), NEG)
    m = jnp.max(s, axis=-1, keepdims=True)
    e = jnp.where(valid, jnp.exp(s - m), 0.0)
    l = jnp.sum(e, axis=-1, keepdims=True)
    p = e * (1.0 / jnp.maximum(l, 1e-30))
    r = _dot(p.reshape(N_HEADS * QBLK, ng).astype(BF16), vc_ref[...]).reshape(N_HEADS, QBLK, GROUP_W)
    o_ref[...] = _pick_heads(r).astype(BF16)

    n_free = NSA_TOP_N - 3
    nv = nslc // 8
    sub = lax.broadcasted_iota(jnp.int32, (8, QBLK), 0)
    jrow = lax.broadcasted_iota(jnp.int32, (nslc, QBLK), 0)
    cur = (q0 + lax.broadcasted_iota(jnp.int32, (nslc, QBLK), 1)) // NSA_SLC_L
    forced = jnp.where(jrow == 0, 1.0, jnp.where(jrow == cur, 1.0, jnp.where(jrow == cur - 1, 1.0, 0.0)))
    causal = jnp.where(jrow <= cur, 1.0, 0.0)
    ovt = ovt_ref[...]
    for hk in range(2):
        psum = p[2 * hk] + p[2 * hk + 1]
        hi, lo = _split_hi_lo(psum)
        imp_t = _dot_nt(ovt, hi) + _dot_nt(ovt, lo)
        val = jnp.where(causal * (1.0 - forced) > 0.5, imp_t, -1.0)
        scr[hk] = val
        vals = [val[8 * v:8 * v + 8, :] for v in range(nv)]
        cnts = [jnp.zeros((8, QBLK), F32) for _ in range(nv)]
        for jp in range(nslc):
            rv = jnp.broadcast_to(scr[hk, jp:jp + 1, :], (8, QBLK))
            for v in range(nv):
                if 8 * v > jp:
                    beats = jnp.where(rv >= vals[v], 1.0, 0.0)
                elif 8 * v + 7 <= jp:
                    beats = jnp.where(rv > vals[v], 1.0, 0.0)
                else:
                    tie = jnp.where(sub + 8 * v > jp, jnp.where(rv == vals[v], 1.0, 0.0), 0.0)
                    beats = jnp.where(rv > vals[v], 1.0, 0.0) + tie
                cnts[v] = cnts[v] + beats
        cnt = jnp.concatenate(cnts, axis=0)
        picked = jnp.where(cnt < n_free - 0.5, 1.0, 0.0)
        sel_t = causal * jnp.maximum(forced, picked)
        sel_t2 = jnp.concatenate([sel_t, sel_t], axis=0).astype(BF16)
        sel_q = _dot_nt(eye_ref[...], sel_t2)
        sb_ref[hk] = jnp.where(sel_q > 0.5, 0.0, NEG).astype(BF16)


def _cmp_sel_call(u3, kc_rep, vc_rep, ovt, eye, bsz, seq):
    ng = seq // NSA_CMP_D
    nslc = seq // NSA_SLC_L
    assert nslc == 64, "selection-bias layout assumes 64 selection blocks"
    return pl.pallas_call(
        functools.partial(_cmp_sel_body, ng=ng, nslc=nslc),
        out_shape=(jax.ShapeDtypeStruct((bsz, seq, GROUP_W), BF16),
                   jax.ShapeDtypeStruct((bsz, 2, seq, 128), BF16)),
        grid=(bsz, seq // QBLK),
        in_specs=[
            pl.BlockSpec((None, QBLK, GROUP_W), lambda b, i: (b, i, COL['b_q'] // GROUP_W)),
            pl.BlockSpec((None, ng, GROUP_W), lambda b, i: (b, 0, 0)),
            pl.BlockSpec((None, ng, GROUP_W), lambda b, i: (b, 0, 0)),
            pl.BlockSpec((nslc, ng), lambda b, i: (0, 0)),
            pl.BlockSpec((QBLK, QBLK), lambda b, i: (0, 0)),
        ],
        out_specs=(pl.BlockSpec((None, QBLK, GROUP_W), lambda b, i: (b, i, 0)),
                   pl.BlockSpec((None, 2, QBLK, 128), lambda b, i: (b, 0, i, 0))),
        scratch_shapes=[pltpu.VMEM((2, nslc, QBLK), F32)],
        compiler_params=_params(2),
        name="nsa_cmp_sel",
    )(u3, kc_rep, vc_rep, ovt, eye)


SLC_KC = 512


def _slc_body(q_ref, sb_ref, ks_ref, vs_ref, o_ref, kaug, vaug, m_s, acc_s, *, seq):
    i = pl.program_id(1)
    q0 = i * QBLK

    @pl.when(i == 0)
    def _():
        ch = 512
        lane = lax.broadcasted_iota(jnp.int32, (ch, 128), 1)
        lo = lane < HEAD_DIM
        for c in range(seq // ch):
            rows = slice(c * ch, (c + 1) * ch)
            r = c * ch + lax.broadcasted_iota(jnp.int32, (ch, 128), 0)
            onehot = jnp.where((lane & (HEAD_DIM - 1)) == r // NSA_SLC_L, 1.0, 0.0)
            kf = ks_ref[rows, :].astype(F32)
            ksw = pltpu.roll(kf, HEAD_DIM, 1)
            vf = vs_ref[rows, :].astype(F32)
            vsw = pltpu.roll(vf, HEAD_DIM, 1)
            one = jnp.ones_like(vf)
            kaug[0, rows, :] = jnp.where(lo, kf, onehot).astype(BF16)
            kaug[1, rows, :] = jnp.where(lo, onehot, ksw).astype(BF16)
            kaug[2, rows, :] = jnp.where(lo, ksw, onehot).astype(BF16)
            kaug[3, rows, :] = jnp.where(lo, onehot, kf).astype(BF16)
            vaug[0, rows, :] = jnp.where(lo, vf, one).astype(BF16)
            vaug[1, rows, :] = jnp.where(lo, one, vsw).astype(BF16)
            vaug[2, rows, :] = jnp.where(lo, vsw, one).astype(BF16)
            vaug[3, rows, :] = jnp.where(lo, one, vf).astype(BF16)

    q = q_ref[...] * jnp.asarray(HEAD_DIM ** -0.5, BF16)
    lane1 = lax.broadcasted_iota(jnp.int32, (1, 128), 1)
    lo1 = lane1 < HEAD_DIM
    qas = []
    for hk in range(2):
        qg = q[:, 128 * hk:128 * hk + 128]
        sbg = sb_ref[hk]
        qas.append(jnp.where(lo1, qg, sbg))
        qas.append(jnp.where(lo1, sbg, qg))

    m_s[...] = jnp.full(m_s.shape, NEG, F32)
    acc_s[...] = jnp.zeros(acc_s.shape, F32)

    def chunk(k0, masked):
        s = jnp.stack([_dot_nt(qas[idx], kaug[idx, pl.ds(k0, SLC_KC), :]) for idx in range(4)])
        if masked:
            row = q0 + lax.broadcasted_iota(jnp.int32, (QBLK, SLC_KC), 0)
            col = k0 + lax.broadcasted_iota(jnp.int32, (QBLK, SLC_KC), 1)
            s = jnp.where((col <= row)[None], s, NEG)
        m_old = m_s[...]
        m_new = jnp.maximum(m_old, jnp.max(s, axis=-1, keepdims=True))
        alpha = jnp.exp(m_old - m_new)
        p = jnp.exp(s - m_new).astype(BF16)
        pv = jnp.stack([_dot(p[idx], vaug[idx, pl.ds(k0, SLC_KC), :]) for idx in range(4)])
        acc_s[...] = acc_s[...] * alpha + pv
        m_s[...] = m_new

    n_full = (q0 + QBLK - 1) // SLC_KC

    def body(c, carry):
        chunk(pl.multiple_of(c * SLC_KC, SLC_KC), False)
        return carry

    lax.fori_loop(0, n_full, body, 0)
    chunk(pl.multiple_of(n_full * SLC_KC, SLC_KC), True)

    acc = acc_s[...]
    for hk in range(2):
        res = []
        for g in range(2):
            a = acc[2 * hk + g]
            is_l = (~lo1) if g == 0 else lo1
            inv = 1.0 / jnp.where(is_l, a, 1.0)
            res.append(a * pltpu.roll(inv, HEAD_DIM, 1))
        o_ref[:, 128 * hk:128 * hk + 128] = jnp.where(lo1, res[0], res[1]).astype(BF16)


def _slc_call(u3, sb, bsz, seq):
    return pl.pallas_call(
        functools.partial(_slc_body, seq=seq),
        out_shape=jax.ShapeDtypeStruct((bsz, seq, GROUP_W), BF16),
        grid=(bsz, seq // QBLK),
        in_specs=[
            pl.BlockSpec((None, QBLK, GROUP_W), lambda b, i: (b, i, COL['b_q'] // GROUP_W)),
            pl.BlockSpec((None, 2, QBLK, 128), lambda b, i: (b, 0, i, 0)),
            pl.BlockSpec((None, seq, 128), lambda b, i: (b, 0, COL['b_ks'] // 128)),
            pl.BlockSpec((None, seq, 128), lambda b, i: (b, 0, COL['b_vs'] // 128)),
        ],
        out_specs=pl.BlockSpec((None, QBLK, GROUP_W), lambda b, i: (b, i, 0)),
        scratch_shapes=[pltpu.VMEM((4, seq, 128), BF16), pltpu.VMEM((4, seq, 128), BF16),
                        pltpu.VMEM((4, QBLK, 1), F32), pltpu.VMEM((4, QBLK, 128), F32)],
        compiler_params=_params(2),
        name="nsa_slc",
    )(u3, sb, u3, u3).reshape(bsz * seq, GROUP_W)


GLA_TC = 256
_GLA_NLEV = 6


def _gla_constants():
    c = GLA_CHUNK
    idx = np.arange(c)
    t = idx[None, :]
    mats = [(t <= idx[:, None])]
    dq, dk, masks = [], [], []
    for lev in range(_GLA_NLEV):
        m = (c // 2) >> lev
        blk = idx // m
        ref_q = blk * m
        ref_k = (blk + 1) * m
        odd = (blk % 2 == 1)
        dq.append(((t > ref_q[:, None]) & (t <= idx[:, None]) & odd[:, None]))
        dk.append(((t > idx[:, None]) & (t <= ref_k[:, None]) & (~odd)[:, None]))
        same_parent = (idx[:, None] // (2 * m)) == (idx[None, :] // (2 * m))
        masks.append(same_parent & odd[:, None] & (~odd)[None, :])
    masks.append(np.eye(c, dtype=bool))
    tail = (t > idx[:, None])
    dall = np.concatenate(mats + dq + dk + [tail], axis=0).astype(np.float32)
    mask4 = np.stack([np.tile(mk.astype(np.float32), (1, N_HEADS)) for mk in masks])
    hv = np.arange(GROUP_W) // HEAD_DIM
    hd = np.arange(N_HEADS * GLA_DK) // GLA_DK
    bd = (hv[:, None] == hd[None, :]).astype(np.float32)
    same_head = (hv[:, None] == hv[None, :]).astype(np.float32)
    return dall, mask4, bd, same_head


def _gla_body(cq_ref, ck_ref, cv_ref, misc_ref, wa_ref, ba_ref, dall_ref, mask_ref, bd_ref,
              sh_ref, gn_ref, o_ref, st):
    j = pl.program_id(1)

    @pl.when(j == 0)
    def _():
        st[...] = jnp.zeros_like(st)

    c = GLA_CHUNK
    nk = N_HEADS * GLA_DK
    lane_h = lax.broadcasted_iota(jnp.int32, (1, nk), 1) // GLA_DK
    lane_v = lax.broadcasted_iota(jnp.int32, (1, GROUP_W), 1) // HEAD_DIM
    dall = dall_ref[...]
    for ch in range(GLA_TC // c):
        rows = slice(ch * c, (ch + 1) * c)
        z = _dot(misc_ref[rows, :], wa_ref[...]) + ba_ref[...]
        la = (jnp.minimum(z, 0.0) - jnp.log1p(jnp.exp(-jnp.abs(z)))) * (1.0 / GLA_TAU)
        hi, lo = _split_hi_lo(la)
        e2 = _dot(dall, jnp.concatenate([hi, lo], axis=1))
        e_all = e2[:, 0:nk] + e2[:, nk:2 * nk]
        bcum = e_all[0:c]
        qf = cq_ref[rows, :].astype(F32) * (GLA_DK ** -0.5)
        kf = ck_ref[rows, :].astype(F32)
        v = cv_ref[rows, :]
        att = jnp.zeros((c, N_HEADS * c), F32)
        for lev in range(_GLA_NLEV + 1):
            if lev < _GLA_NLEV:
                ql = qf * jnp.exp(e_all[(1 + lev) * c:(2 + lev) * c])
                kl = kf * jnp.exp(e_all[(1 + _GLA_NLEV + lev) * c:(2 + _GLA_NLEV + lev) * c])
            else:
                ql, kl = qf, kf
            klb = kl.astype(BF16)
            kbd = jnp.concatenate([jnp.where(lane_h == h, klb, jnp.zeros_like(klb))
                                   for h in range(N_HEADS)], axis=0)
            att = att + _dot_nt(ql.astype(BF16), kbd) * mask_ref[lev]
        vbd = jnp.concatenate([jnp.where(lane_v == h, v, jnp.zeros_like(v))
                               for h in range(N_HEADS)], axis=0)
        st_old = st[...]
        o = _dot_nt((qf * jnp.exp(bcum)).astype(BF16), st_old.astype(BF16)) + _dot(att.astype(BF16), vbd)
        kt = (kf * jnp.exp(e_all[(1 + 2 * _GLA_NLEV) * c:(2 + 2 * _GLA_NLEV) * c])).astype(BF16)
        st[...] = st_old * jnp.exp(bcum[c - 1:c, :]) + bd_ref[...] * _dot_tn(v, kt)
        ms = _dot_hl(o * o, sh_ref[...]) * (1.0 / HEAD_DIM)
        o_ref[rows, :] = (o * lax.rsqrt(ms + EPS) * gn_ref[...]).astype(BF16)


def _gla_call(u3, wa_emb, ba, gnorm, bsz, seq):
    dall, mask4, bd, same_head = _gla_constants()
    tc = GLA_TC
    full = lambda shape: pl.BlockSpec(shape, lambda b, j: (0,) * len(shape))
    return pl.pallas_call(
        _gla_body,
        out_shape=jax.ShapeDtypeStruct((bsz, seq, GROUP_W), BF16),
        grid=(bsz, seq // tc),
        in_specs=[
            pl.BlockSpec((None, tc, 128), lambda b, j: (b, j, COL['c_q'] // 128)),
            pl.BlockSpec((None, tc, 128), lambda b, j: (b, j, COL['c_k'] // 128)),
            pl.BlockSpec((None, tc, GROUP_W), lambda b, j: (b, j, COL['c_v'] // GROUP_W)),
            pl.BlockSpec((None, tc, 128), lambda b, j: (b, j, COL['misc'] // 128)),
            full((128, 128)), full((1, 128)), full(dall.shape), full(mask4.shape),
            full(bd.shape), full(same_head.shape), full((1, GROUP_W)),
        ],
        out_specs=pl.BlockSpec((None, tc, GROUP_W), lambda b, j: (b, j, 0)),
        scratch_shapes=[pltpu.VMEM((GROUP_W, N_HEADS * GLA_DK), F32)],
        compiler_params=_params(2),
        name="gla",
    )(u3, u3, u3, u3, wa_emb, ba, jnp.asarray(dall, BF16), jnp.asarray(mask4, F32),
      jnp.asarray(bd, F32), jnp.asarray(same_head, BF16), gnorm).reshape(bsz * seq, GROUP_W)


OUT_TM = 512


def _out_body(x_ref, mod_ref, ln_ref, z_ref, o1_ref, o2_ref, o3_ref, l1_ref, l2_ref, l3_ref,
              ocmp_ref, oslc_ref, owin_ref, oc_ref, od_ref, misc_ref, gb_ref, e4_ref, eg_ref,
              w_ref, out_ref):
    l1, l2, l3 = l1_ref[...], l2_ref[...], l3_ref[...]
    mx = jnp.maximum(jnp.maximum(l1, l2), l3)
    e1, e2, e3 = jnp.exp(l1 - mx), jnp.exp(l2 - mx), jnp.exp(l3 - mx)
    inv = 1.0 / (e1 + e2 + e3)
    e4 = e4_ref[...]
    oa = jnp.zeros((OUT_TM, GROUP_W), F32)
    for e, o_ref in ((e1, o1_ref), (e2, o2_ref), (e3, o3_ref)):
        oa = oa + _dot_hl2(e * inv, e4) * o_ref[...].astype(F32)
    sg = jax.nn.sigmoid(misc_ref[...].astype(F32) + gb_ref[...])
    sg_hi, sg_lo = _split_hi_lo(sg)
    sg2 = jnp.concatenate([sg_hi, sg_lo], axis=1)
    ob = jnp.zeros((OUT_TM, GROUP_W), F32)
    for br, o_ref in enumerate((ocmp_ref, oslc_ref, owin_ref)):
        ob = ob + _dot(sg2, eg_ref[br]) * o_ref[...].astype(F32)
    z = z_ref[...].astype(F32)
    sz = z * jax.nn.sigmoid(z)
    mixed = jnp.concatenate([oa, ob, oc_ref[...].astype(F32), od_ref[...].astype(F32)], axis=1) * sz
    y = _dot(mixed.astype(BF16), w_ref[...])
    ms = jnp.mean(y * y, axis=-1, keepdims=True)
    yn = y * lax.rsqrt(ms + EPS) * ln_ref[...]
    out_ref[...] = x_ref[...] + mod_ref[2:3, :] * yn


def _out_call(x2, mod3, ln_post, u2, o1, o2, o3, l1, l2, l3, ocmp, oslc, owin, oc, od,
              gate_b_row, e4, eg, w_out, seq):
    t, d = x2.shape
    tm = OUT_TM
    per_b = seq // tm
    row = lambda w: pl.BlockSpec((tm, w), lambda i: (i, 0))
    full = lambda shape: pl.BlockSpec(shape, lambda i: (0,) * len(shape))
    return pl.pallas_call(
        _out_body,
        out_shape=jax.ShapeDtypeStruct((t, d), F32),
        grid=(t // tm,),
        in_specs=[
            row(d),
            pl.BlockSpec((None, 3, d), lambda i: (i // per_b, 0, 0)),
            full((1, d)),
            pl.BlockSpec((tm, 1024), lambda i: (i, COL['z'] // 1024)),
            row(GROUP_W), row(GROUP_W), row(GROUP_W), row(128), row(128), row(128),
            row(GROUP_W), row(GROUP_W), row(GROUP_W), row(GROUP_W), row(GROUP_W),
            pl.BlockSpec((tm, 128), lambda i: (i, COL['misc'] // 128)),
            full((1, 128)), full((256, GROUP_W)), full((3, 256, GROUP_W)), full((d, d)),
        ],
        out_specs=row(d),
        compiler_params=_params(1),
        name="mix_out",
    )(x2, mod3, ln_post.reshape(1, d), u2, o1, o2, o3, l1, l2, l3, ocmp, oslc, owin, oc, od,
      u2, gate_b_row, e4, eg, w_out)


def _expand_w1(w1):
    w = w1.reshape(2, NSA_CMP_D, HEAD_DIM, NSA_CMP_HID)
    zero = jnp.zeros((NSA_CMP_D, HEAD_DIM, NSA_CMP_HID), w1.dtype)
    top = jnp.concatenate([w[0], w[1], zero, zero], axis=-1)
    bot = jnp.concatenate([zero, zero, w[0], w[1]], axis=-1)
    return jnp.concatenate([top, bot], axis=1).astype(BF16)


def _expand_w2(w2):
    z = jnp.zeros_like(w2)
    h0 = jnp.concatenate([w2, w2, z, z], axis=1)
    h1 = jnp.concatenate([z, z, w2, w2], axis=1)
    return jnp.stack([h0, h1]).astype(BF16)


def _static_tables(seq):
    ng = seq // NSA_CMP_D
    nslc = seq // NSA_SLC_L
    cs = np.arange(ng) * NSA_CMP_D
    ss = np.arange(nslc) * NSA_SLC_L
    ovt = ((cs[None, :] < ss[:, None] + NSA_SLC_L) & (cs[None, :] + NSA_CMP_L > ss[:, None]))
    ovt[:, ng - 1] = False
    e4 = np.zeros((128, GROUP_W), np.float32)
    eg = np.zeros((3, 128, GROUP_W), np.float32)
    for h in range(N_HEADS):
        e4[32 * h, HEAD_DIM * h:HEAD_DIM * (h + 1)] = 1.0
        for br in range(3):
            eg[br, MISC_G_OFF + 3 * h + br, HEAD_DIM * h:HEAD_DIM * (h + 1)] = 1.0
    e4 = np.concatenate([e4, e4], axis=0)
    eg = np.concatenate([eg, eg], axis=1)
    return (jnp.asarray(ovt, BF16), jnp.asarray(np.eye(QBLK), BF16),
            jnp.asarray(e4, BF16), jnp.asarray(eg, BF16))


def kernel(x, c, ln_pre, ln_post, w_ada, b_ada, w_in, w_out, nsa_pos_k, nsa_pos_v, nsa_w1_k, nsa_b1_k, nsa_w2_k, nsa_w1_v, nsa_b1_v, nsa_w2_v, nsa_gate_b, gla_w_a2, gla_b_a, gla_norm, sinks):
    bsz, seq, d = x.shape
    depth = w_in.shape[0]
    t = bsz * seq

    src = jnp.asarray(np.maximum(_SRC_COLS, 0))
    keep = jnp.asarray((_SRC_COLS >= 0).astype(np.float32))
    w_in_r = (jnp.take(w_in, src, axis=2) * keep).astype(BF16)
    w_out_b = w_out.astype(BF16)
    ovt, eye, e4, eg = _static_tables(seq)

    mod = _ada_call(c, w_ada, b_ada).reshape(depth, bsz, 3, d)

    x2 = x.reshape(t, d)
    for i in range(depth):
        u2, ua, uc = _proj_in_call(x2, mod[i], ln_pre[i], w_in_r[i], seq)
        u3 = u2.reshape(bsz, seq, U_W)

        dil = [_dil_pair_call(ua, bsz, seq, w, r) for w, r in DIL_PAIRS]

        kc_rep, vc_rep = _compress_call(
            uc, bsz, seq, _expand_w1(nsa_w1_k[i]), _expand_w1(nsa_w1_v[i]),
            jnp.broadcast_to(nsa_pos_k[i].reshape(1, -1), (8, NSA_CMP_L * HEAD_DIM)).astype(BF16),
            jnp.broadcast_to(nsa_pos_v[i].reshape(1, -1), (8, NSA_CMP_L * HEAD_DIM)).astype(BF16),
            nsa_w1_k[i].astype(BF16), nsa_w1_v[i].astype(BF16),
            nsa_b1_k[i].reshape(1, -1), nsa_b1_v[i].reshape(1, -1),
            _expand_w2(nsa_w2_k[i]), _expand_w2(nsa_w2_v[i]))
        ocmp, sb = _cmp_sel_call(u3, kc_rep, vc_rep, ovt, eye, bsz, seq)
        oslc = _slc_call(u3, sb, bsz, seq)
        owin = _win_call(u3, bsz, seq)

        wa_emb = jnp.zeros((128, 128), F32).at[MISC_A_OFF:MISC_A_OFF + GLA_RANK, :].set(gla_w_a2[i]).astype(BF16)
        oc = _gla_call(u3, wa_emb, gla_b_a[i].reshape(1, -1), gla_norm[i].reshape(1, -1), bsz, seq)

        od = _swa_call(u3, sinks[i], bsz, seq)

        gate_b_row = jnp.zeros((1, 128), F32).at[0, MISC_G_OFF:MISC_G_OFF + 3 * N_HEADS].set(nsa_gate_b[i])
        x2 = _out_call(x2, mod[i], ln_post[i], u2, dil[0][0], dil[1][0], dil[2][0],
                       dil[0][1], dil[1][1], dil[2][1], ocmp.reshape(t, GROUP_W), oslc, owin, oc, od,
                       gate_b_row, e4, eg, w_out_b[i], seq)
    return x2.reshape(bsz, seq, d)
```

```python
import functools

import numpy as np
import jax
import jax.numpy as jnp
from jax import lax
from jax.experimental import pallas as pl
from jax.experimental.pallas import tpu as pltpu

F32 = jnp.float32
BF16 = jnp.bfloat16

D_MODEL = 1024
HEAD_DIM = 64
N_HEADS = 4
GROUP_W = N_HEADS * HEAD_DIM
EPS = 1e-6
NEG = -1e30
QBLK = 128

DIL_PAIRS = ((128, 1), (512, 4), (2048, 16))
NSA_CMP_L = 32
NSA_CMP_D = 16
NSA_CMP_HID = 128
NSA_SLC_L = 64
NSA_TOP_N = 16
NSA_WIN = 512
GLA_DK = 32
GLA_RANK = 16
GLA_TAU = 16.0
GLA_CHUNK = 64
SWA_WIN = 128

VMEM_LIMIT = 56 * 1024 * 1024

_ORIG_SPLITS = (
    ('a_q', 256), ('a_k', 256), ('a_v', 256), ('a_z', 256),
    ('b_q', 256), ('b_kc', 128), ('b_vc', 128), ('b_ks', 128), ('b_vs', 128),
    ('b_kw', 128), ('b_vw', 128), ('b_g', 12), ('b_z', 256),
    ('c_q', 128), ('c_k', 128), ('c_v', 256), ('c_a', 16), ('c_z', 256),
    ('d_q', 256), ('d_k', 64), ('d_v', 64), ('d_z', 256),
)
_NEW_ORDER = (
    ('a_q', ('a_q',), 256), ('a_k', ('a_k',), 256), ('a_v', ('a_v',), 256),
    ('b_kc', ('b_kc',), 128), ('b_vc', ('b_vc',), 128),
    ('z', ('a_z', 'b_z', 'c_z', 'd_z'), 1024),
    ('b_q', ('b_q',), 256), ('d_q', ('d_q',), 256), ('c_v', ('c_v',), 256),
    ('b_ks', ('b_ks',), 128), ('b_vs', ('b_vs',), 128), ('b_kw', ('b_kw',), 128),
    ('b_vw', ('b_vw',), 128), ('c_q', ('c_q',), 128), ('c_k', ('c_k',), 128),
    ('d_kv', ('d_k', 'd_v'), 128), ('misc', ('b_g', 'c_a'), 128),
)
A_W = 768
C_W = 256
MISC_G_OFF = 0
MISC_A_OFF = 12


def _build_layout():
    off, o = {}, 0
    for name, w in _ORIG_SPLITS:
        off[name] = (o, w)
        o += w
    src, col, n = [], {}, 0
    for name, parts, width in _NEW_ORDER:
        col[name] = n
        used = 0
        for p in parts:
            s, w = off[p]
            src.extend(range(s, s + w))
            used += w
        src.extend([-1] * (width - used))
        n += width
    return np.asarray(src, np.int32), col, n


_SRC_COLS, COL, U_W = _build_layout()
assert COL['a_q'] == 0 and COL['b_kc'] == A_W and COL['z'] == A_W + C_W


def _dot(a, b):
    return jnp.dot(a, b, preferred_element_type=F32)


def _dot_nt(a, b):
    return lax.dot_general(a, b, (((1,), (1,)), ((), ())), preferred_element_type=F32)


def _dot_tn(a, b):
    return lax.dot_general(a, b, (((0,), (0,)), ((), ())), preferred_element_type=F32)


def _split_hi_lo(x):
    hi = x.astype(BF16)
    lo = (x - hi.astype(F32)).astype(BF16)
    return hi, lo


def _dot_hl(x, w):
    hi, lo = _split_hi_lo(x)
    return _dot(hi, w) + _dot(lo, w)


def _dot_hl2(x, w2):
    hi, lo = _split_hi_lo(x)
    return _dot(jnp.concatenate([hi, lo], axis=1), w2)


def _params(n_axes):
    return pltpu.CompilerParams(dimension_semantics=("arbitrary",) * n_axes,
                                vmem_limit_bytes=VMEM_LIMIT)


def _ada_body(c_ref, w_ref, b_ref, o_ref):
    c = c_ref[...]
    sc = (c * jax.nn.sigmoid(c)).astype(BF16)
    o_ref[...] = _dot(sc, w_ref[...].astype(BF16)) + b_ref[...]


def _ada_call(c, w_ada, b_ada):
    depth, d, d3 = w_ada.shape
    b = c.shape[0]
    nb = d3 // d
    return pl.pallas_call(
        _ada_body,
        out_shape=jax.ShapeDtypeStruct((depth, b, d3), F32),
        grid=(depth, nb),
        in_specs=[
            pl.BlockSpec((b, d), lambda i, n: (0, 0)),
            pl.BlockSpec((None, d, d), lambda i, n: (i, 0, n)),
            pl.BlockSpec((None, 1, d), lambda i, n: (i, 0, n)),
        ],
        out_specs=pl.BlockSpec((None, b, d), lambda i, n: (i, 0, n)),
        compiler_params=_params(2),
        name="ada_mod",
    )(c, w_ada, b_ada.reshape(depth, 1, d3))


PROJ_TM = 512
_PROJ_CHUNKS = ((0, 768), (768, 1024), (1024, 1792), (1792, 2560), (2560, 3328), (3328, 3840))


def _proj_in_body(x_ref, mod_ref, ln_ref, w_ref, o_ref, oa_ref, oc_ref):
    x = x_ref[...]
    ms = jnp.mean(x * x, axis=-1, keepdims=True)
    y = x * lax.rsqrt(ms + EPS) * ln_ref[...]
    shift = mod_ref[0:1, :]
    scale = mod_ref[1:2, :]
    h = (y * (1.0 + scale) + shift).astype(BF16)
    for n0, n1 in _PROJ_CHUNKS:
        r = _dot(h, w_ref[:, n0:n1]).astype(BF16)
        o_ref[:, n0:n1] = r
        if (n0, n1) == (0, A_W):
            oa_ref[...] = r
        elif (n0, n1) == (A_W, A_W + C_W):
            oc_ref[...] = r


def _proj_in_call(x2, mod3, ln_pre, w_r, seq):
    t, d = x2.shape
    tm = PROJ_TM
    per_b = seq // tm
    return pl.pallas_call(
        _proj_in_body,
        out_shape=(jax.ShapeDtypeStruct((t, U_W), BF16),
                   jax.ShapeDtypeStruct((t, A_W), BF16),
                   jax.ShapeDtypeStruct((t, C_W), BF16)),
        grid=(t // tm,),
        in_specs=[
            pl.BlockSpec((tm, d), lambda i: (i, 0)),
            pl.BlockSpec((None, 3, d), lambda i: (i // per_b, 0, 0)),
            pl.BlockSpec((1, d), lambda i: (0, 0)),
            pl.BlockSpec((d, U_W), lambda i: (0, 0)),
        ],
        out_specs=(pl.BlockSpec((tm, U_W), lambda i: (i, 0)),
                   pl.BlockSpec((tm, A_W), lambda i: (i, 0)),
                   pl.BlockSpec((tm, C_W), lambda i: (i, 0))),
        compiler_params=_params(1),
        name="proj_in",
    )(x2, mod3, ln_pre.reshape(1, d), w_r)


def _stack_heads(q):
    lane_h = lax.broadcasted_iota(jnp.int32, (1, GROUP_W), 1) // HEAD_DIM
    zero = jnp.zeros_like(q)
    return jnp.concatenate([jnp.where(lane_h == h, q, zero) for h in range(N_HEADS)], axis=0)


def _pick_heads(r):
    lane_h = lax.broadcasted_iota(jnp.int32, (1, GROUP_W), 1) // HEAD_DIM
    out = jnp.where(lane_h == 0, r[0], 0.0)
    for h in range(1, N_HEADS):
        out = jnp.where(lane_h == h, r[h], out)
    return out


def _band_bias_table(blk, span, nb, back):
    d = np.arange(nb + 1)[:, None, None] * blk + np.arange(blk)[None, :, None] - np.arange(span)[None, None, :]
    return jnp.asarray(np.where((d >= 0) & (d <= back), 0.0, NEG), F32)


def _banded_core(i, q_ref, k_src, v_src, bias_ref, o_ref, lse_ref, sink_ref, *, blk, nsub, span, nb, back):
    if sink_ref is not None:
        hidx = lax.broadcasted_iota(jnp.int32, (N_HEADS, 1, 1), 0)
        sink = jnp.zeros((N_HEADS, 1, 1), F32)
        for h in range(N_HEADS):
            sink = jnp.where(hidx == h, sink_ref[h], sink)
    lane_l = lax.broadcasted_iota(jnp.int32, (1, 128), 1) // 32

    def window(sb):
        ib = i * nsub + sb
        return ib, pl.multiple_of(jnp.maximum(ib - nb, 0) * blk, blk)

    def qk(sb):
        ib, start = window(sb)
        q = q_ref[sb * blk:(sb + 1) * blk, :] * jnp.asarray(HEAD_DIM ** -0.5, BF16)
        s = _dot_nt(_stack_heads(q), k_src[pl.ds(start, span), :]).reshape(N_HEADS, blk, span)
        return s + bias_ref[jnp.minimum(ib, nb)][None]

    s_next = qk(0)
    for sb in range(nsub):
        s = s_next
        s_next = qk(sb + 1) if sb + 1 < nsub else None
        vb = v_src[pl.ds(window(sb)[1], span), :]
        m = jnp.max(s, axis=-1, keepdims=True)
        if sink_ref is not None:
            m = jnp.maximum(m, sink)
        p = jnp.exp(s - m)
        l = jnp.sum(p, axis=-1, keepdims=True)
        den = l if sink_ref is None else l + jnp.exp(sink - m)
        r = _dot(p.reshape(N_HEADS * blk, span).astype(BF16), vb).reshape(N_HEADS, blk, GROUP_W)
        o_ref[sb * blk:(sb + 1) * blk, :] = _pick_heads(r * (1.0 / den)).astype(o_ref.dtype)
        if lse_ref is not None:
            lse = m + jnp.log(l)
            acc = jnp.zeros((blk, 128), F32)
            for h in range(N_HEADS):
                acc = jnp.where(lane_l == h, lse[h], acc)
            lse_ref[sb * blk:(sb + 1) * blk, :] = acc


def _dil_body(q_ref, k_ref, v_ref, bias_ref, o_ref, lse_ref, **kw):
    _banded_core(pl.program_id(2), q_ref, k_ref, v_ref, bias_ref, o_ref, lse_ref, None, **kw)


def _banded_geometry(ln, back):
    blk = min(QBLK, ln)
    nb = -(-back // blk)
    span = min((nb + 1) * blk, ln)
    nsub = 4 if ln % (4 * blk) == 0 else (2 if ln % (2 * blk) == 0 else 1)
    return blk, nb, span, nsub


def _dil_pair_call(ua, bsz, seq, w, r):
    ln = seq // r
    back = w // r
    blk, nb, span, nsub = _banded_geometry(ln, back)
    rows = blk * nsub
    uv = ua.reshape(bsz, ln, r * A_W)
    cpb = A_W // GROUP_W
    o, lse = pl.pallas_call(
        functools.partial(_dil_body, blk=blk, nsub=nsub, span=span, nb=nb, back=back),
        out_shape=(jax.ShapeDtypeStruct((bsz, ln, r * GROUP_W), BF16),
                   jax.ShapeDtypeStruct((bsz, ln, r * 128), F32)),
        grid=(bsz, r, ln // rows),
        in_specs=[
            pl.BlockSpec((None, rows, GROUP_W), lambda b, j, i: (b, i, j * cpb)),
            pl.BlockSpec((None, ln, GROUP_W), lambda b, j, i: (b, 0, j * cpb + 1)),
            pl.BlockSpec((None, ln, GROUP_W), lambda b, j, i: (b, 0, j * cpb + 2)),
            pl.BlockSpec((nb + 1, blk, span), lambda b, j, i: (0, 0, 0)),
        ],
        out_specs=(pl.BlockSpec((None, rows, GROUP_W), lambda b, j, i: (b, i, j)),
                   pl.BlockSpec((None, rows, 128), lambda b, j, i: (b, i, j))),
        compiler_params=_params(3),
        name=f"dil_r{r}",
    )(uv, uv, uv, _band_bias_table(blk, span, nb, back))
    return o.reshape(bsz * seq, GROUP_W), lse.reshape(bsz * seq, 128)


def _rep_prep(src_refs, dst_refs, seq, mode):
    ch = 512
    lane = lax.broadcasted_iota(jnp.int32, (ch, 128), 1)
    lo = lane < HEAD_DIM
    for c in range(seq // ch):
        rows = slice(c * ch, (c + 1) * ch)
        if mode == 'swa':
            x = src_refs[0][rows, :].astype(F32)
            sw = pltpu.roll(x, HEAD_DIM, 1)
            kk = jnp.where(lo, x, sw).astype(BF16)
            vv = jnp.where(lo, sw, x).astype(BF16)
            dst_refs[0][rows, 0:128] = kk
            dst_refs[0][rows, 128:256] = kk
            dst_refs[1][rows, 0:128] = vv
            dst_refs[1][rows, 128:256] = vv
        else:
            for s_ref, d_ref in zip(src_refs, dst_refs):
                x = s_ref[rows, :].astype(F32)
                sw = pltpu.roll(x, HEAD_DIM, 1)
                d_ref[rows, 0:128] = jnp.where(lo, x, sw).astype(BF16)
                d_ref[rows, 128:256] = jnp.where(lo, sw, x).astype(BF16)


def _swa_body(sink_ref, q_ref, kv_ref, bias_ref, o_ref, kr, vr, *, seq, **kw):
    i = pl.program_id(1)

    @pl.when(i == 0)
    def _():
        _rep_prep((kv_ref,), (kr, vr), seq, 'swa')

    _banded_core(i, q_ref, kr, vr, bias_ref, o_ref, None, sink_ref, **kw)


def _win_body(q_ref, kw_ref, vw_ref, bias_ref, o_ref, kr, vr, *, seq, **kw):
    i = pl.program_id(1)

    @pl.when(i == 0)
    def _():
        _rep_prep((kw_ref, vw_ref), (kr, vr), seq, 'win')

    _banded_core(i, q_ref, kr, vr, bias_ref, o_ref, None, None, **kw)


def _swa_call(u3, sinks, bsz, seq):
    back = SWA_WIN - 1
    blk, nb, span, nsub = _banded_geometry(seq, back)
    rows = blk * nsub
    return pl.pallas_call(
        functools.partial(_swa_body, seq=seq, blk=blk, nsub=nsub, span=span, nb=nb, back=back),
        out_shape=jax.ShapeDtypeStruct((bsz, seq, GROUP_W), BF16),
        grid=(bsz, seq // rows),
        in_specs=[
            pl.BlockSpec(memory_space=pltpu.SMEM),
            pl.BlockSpec((None, rows, GROUP_W), lambda b, i: (b, i, COL['d_q'] // GROUP_W)),
            pl.BlockSpec((None, seq, 128), lambda b, i: (b, 0, COL['d_kv'] // 128)),
            pl.BlockSpec((nb + 1, blk, span), lambda b, i: (0, 0, 0)),
        ],
        out_specs=pl.BlockSpec((None, rows, GROUP_W), lambda b, i: (b, i, 0)),
        scratch_shapes=[pltpu.VMEM((seq, GROUP_W), BF16), pltpu.VMEM((seq, GROUP_W), BF16)],
        compiler_params=_params(2),
        name="swa",
    )(sinks, u3, u3, _band_bias_table(blk, span, nb, back)).reshape(bsz * seq, GROUP_W)


def _win_call(u3, bsz, seq):
    back = NSA_WIN - 1
    blk, nb, span, nsub = _banded_geometry(seq, back)
    rows = blk * nsub
    return pl.pallas_call(
        functools.partial(_win_body, seq=seq, blk=blk, nsub=nsub, span=span, nb=nb, back=back),
        out_shape=jax.ShapeDtypeStruct((bsz, seq, GROUP_W), BF16),
        grid=(bsz, seq // rows),
        in_specs=[
            pl.BlockSpec((None, rows, GROUP_W), lambda b, i: (b, i, COL['b_q'] // GROUP_W)),
            pl.BlockSpec((None, seq, 128), lambda b, i: (b, 0, COL['b_kw'] // 128)),
            pl.BlockSpec((None, seq, 128), lambda b, i: (b, 0, COL['b_vw'] // 128)),
            pl.BlockSpec((nb + 1, blk, span), lambda b, i: (0, 0, 0)),
        ],
        out_specs=pl.BlockSpec((None, rows, GROUP_W), lambda b, i: (b, i, 0)),
        scratch_shapes=[pltpu.VMEM((seq, GROUP_W), BF16), pltpu.VMEM((seq, GROUP_W), BF16)],
        compiler_params=_params(2),
        name="nsa_win",
    )(u3, u3, u3, _band_bias_table(blk, span, nb, back)).reshape(bsz * seq, GROUP_W)


def _compress_body(xk_ref, xv_ref, wk_ref, wv_ref, pk_ref, pv_ref, w1k_ref, w1v_ref,
                   b1k_ref, b1v_ref, w2k_ref, w2v_ref, ok_ref, ov_ref, acck, accv, *, ng):
    l = pl.program_id(1)

    @pl.when(l == 0)
    def _():
        acck[...] = jnp.zeros_like(acck)
        accv[...] = jnp.zeros_like(accv)

    acck[...] += _dot(xk_ref[...], wk_ref[...])
    accv[...] += _dot(xv_ref[...], wv_ref[...])

    @pl.when(l == NSA_CMP_D - 1)
    def _():
        for acc, p_ref, w1_ref, b1_ref, w2_ref, o_ref in (
                (acck, pk_ref, w1k_ref, b1k_ref, w2k_ref, ok_ref),
                (accv, pv_ref, w1v_ref, b1v_ref, w2v_ref, ov_ref)):
            const = _dot(p_ref[...], w1_ref[...])[0:1, :] + b1_ref[...]
            out = jnp.zeros((ng, GROUP_W), F32)
            for h in range(2):
                first = acc[:, 256 * h:256 * h + 128]
                second = acc[:, 256 * h + 128:256 * h + 256]
                pre = first + pltpu.roll(second, ng - 1, 0) + const
                act = jax.nn.gelu(pre, approximate=True)
                out = out + _dot(act.astype(BF16), w2_ref[h])
            o_ref[...] = out.astype(BF16)


def _compress_call(uc, bsz, seq, wk_exp, wv_exp, posk, posv, w1k, w1v, b1k, b1v, w2k_exp, w2v_exp):
    ng = seq // NSA_CMP_D
    uv = uc.reshape(bsz, ng, NSA_CMP_D * C_W)
    cpb = C_W // 128
    full2 = lambda shape: pl.BlockSpec(shape, lambda b, l: (0,) * len(shape))
    return pl.pallas_call(
        functools.partial(_compress_body, ng=ng),
        out_shape=(jax.ShapeDtypeStruct((bsz, ng, GROUP_W), BF16),
                   jax.ShapeDtypeStruct((bsz, ng, GROUP_W), BF16)),
        grid=(bsz, NSA_CMP_D),
        in_specs=[
            pl.BlockSpec((None, ng, 128), lambda b, l: (b, 0, l * cpb)),
            pl.BlockSpec((None, ng, 128), lambda b, l: (b, 0, l * cpb + 1)),
            pl.BlockSpec((None, 128, 512), lambda b, l: (l, 0, 0)),
            pl.BlockSpec((None, 128, 512), lambda b, l: (l, 0, 0)),
            full2((8, NSA_CMP_L * HEAD_DIM)), full2((8, NSA_CMP_L * HEAD_DIM)),
            full2((NSA_CMP_L * HEAD_DIM, NSA_CMP_HID)), full2((NSA_CMP_L * HEAD_DIM, NSA_CMP_HID)),
            full2((1, NSA_CMP_HID)), full2((1, NSA_CMP_HID)),
            full2((2, NSA_CMP_HID, GROUP_W)), full2((2, NSA_CMP_HID, GROUP_W)),
        ],
        out_specs=(pl.BlockSpec((None, ng, GROUP_W), lambda b, l: (b, 0, 0)),
                   pl.BlockSpec((None, ng, GROUP_W), lambda b, l: (b, 0, 0))),
        scratch_shapes=[pltpu.VMEM((ng, 512), F32), pltpu.VMEM((ng, 512), F32)],
        compiler_params=_params(2),
        name="nsa_compress",
    )(uv, uv, wk_exp, wv_exp, posk, posv, w1k, w1v, b1k, b1v, w2k_exp, w2v_exp)


def _cmp_sel_body(q_ref, kc_ref, vc_ref, ovt_ref, o_ref, sb_ref, scr, *, ng, nslc):
    i = pl.program_id(1)
    q0 = i * QBLK
    q = q_ref[...] * jnp.asarray(HEAD_DIM ** -0.5, BF16)
    t = q0 + lax.broadcasted_iota(jnp.int32, (QBLK, ng), 0)
    ci = lax.broadcasted_iota(jnp.int32, (QBLK, ng), 1)
    valid = ((ci * NSA_CMP_D + (NSA_CMP_L - 1)) <= t)[None]
    s = jnp.where(valid, _dot_nt(_stack_heads(q), kc_ref[...]).reshape(N_HEADS, QBLK, ng), NEG)
    m = jnp.max(s, axis=-1, keepdims=True)
    e = jnp.where(valid, jnp.exp(s - m), 0.0)
    l = jnp.sum(e, axis=-1, keepdims=True)
    p = e * (1.0 / jnp.maximum(l, 1e-30))
    r = _dot(p.reshape(N_HEADS * QBLK, ng).astype(BF16), vc_ref[...]).reshape(N_HEADS, QBLK, GROUP_W)
    o_ref[...] = _pick_heads(r).astype(BF16)

    n_free = NSA_TOP_N - 3
    nv = nslc // 8
    sub = lax.broadcasted_iota(jnp.int32, (8, QBLK), 0)
    jrow = lax.broadcasted_iota(jnp.int32, (nslc, QBLK), 0)
    cur = (q0 + lax.broadcasted_iota(jnp.int32, (nslc, QBLK), 1)) // NSA_SLC_L
    forced = jnp.where(jrow == 0, 1.0, jnp.where(jrow == cur, 1.0, jnp.where(jrow == cur - 1, 1.0, 0.0)))
    causal = jnp.where(jrow <= cur, 1.0, 0.0)
    ovt = ovt_ref[...]
    for hk in range(2):
        psum = p[2 * hk] + p[2 * hk + 1]
        hi, lo = _split_hi_lo(psum)
        imp_t = _dot_nt(ovt, hi) + _dot_nt(ovt, lo)
        scr[hk] = jnp.where(causal * (1.0 - forced) > 0.5, imp_t, -1.0)

    tier = (q0 + QBLK - 1) // (8 * NSA_SLC_L)
    for tt in range(nv):
        @pl.when(tier == tt)
        def _(tt=tt):
            nvt = tt + 1
            for hk in range(2):
                vals = [scr[hk, 8 * v:8 * v + 8, :] for v in range(nvt)]
                cnts = [jnp.zeros((8, QBLK), F32) for _ in range(nvt)]
                for jp in range(8 * nvt):
                    rv = jnp.broadcast_to(scr[hk, jp:jp + 1, :], (8, QBLK))
                    for v in range(nvt):
                        if 8 * v > jp:
                            beats = jnp.where(rv >= vals[v], 1.0, 0.0)
                        elif 8 * v + 7 <= jp:
                            beats = jnp.where(rv > vals[v], 1.0, 0.0)
                        else:
                            tie = jnp.where(sub + 8 * v > jp, jnp.where(rv == vals[v], 1.0, 0.0), 0.0)
                            beats = jnp.where(rv > vals[v], 1.0, 0.0) + tie
                        cnts[v] = cnts[v] + beats
                pieces = [jnp.where(c < n_free - 0.5, 1.0, 0.0) for c in cnts]
                if nvt < nv:
                    pieces.append(jnp.zeros((8 * (nv - nvt), QBLK), F32))
                picked = jnp.concatenate(pieces, axis=0)
                sel_t = causal * jnp.maximum(forced, picked)
                sb_ref[hk] = jnp.where(sel_t > 0.5, 0.0, NEG).astype(BF16)


def _cmp_sel_call(u3, kc_rep, vc_rep, ovt, bsz, seq):
    ng = seq // NSA_CMP_D
    nslc = seq // NSA_SLC_L
    assert nslc == 64, "selection-bias layout assumes 64 selection blocks"
    return pl.pallas_call(
        functools.partial(_cmp_sel_body, ng=ng, nslc=nslc),
        out_shape=(jax.ShapeDtypeStruct((bsz, seq, GROUP_W), BF16),
                   jax.ShapeDtypeStruct((bsz, 2, nslc, seq), BF16)),
        grid=(bsz, seq // QBLK),
        in_specs=[
            pl.BlockSpec((None, QBLK, GROUP_W), lambda b, i: (b, i, COL['b_q'] // GROUP_W)),
            pl.BlockSpec((None, ng, GROUP_W), lambda b, i: (b, 0, 0)),
            pl.BlockSpec((None, ng, GROUP_W), lambda b, i: (b, 0, 0)),
            pl.BlockSpec((nslc, ng), lambda b, i: (0, 0)),
        ],
        out_specs=(pl.BlockSpec((None, QBLK, GROUP_W), lambda b, i: (b, i, 0)),
                   pl.BlockSpec((None, 2, nslc, QBLK), lambda b, i: (b, 0, 0, i))),
        scratch_shapes=[pltpu.VMEM((2, nslc, QBLK), F32)],
        compiler_params=_params(2),
        name="nsa_cmp_sel",
    )(u3, kc_rep, vc_rep, ovt)


SLC_KC = 512
SLC_QB = 256


def _slc_body(q_ref, sb_ref, ks_ref, vs_ref, o_ref, kaug, vaugt, m_s, acc_s, *, seq):
    i = pl.program_id(1)
    qb = SLC_QB
    q0 = i * qb
    kc = SLC_KC

    @pl.when(i == 0)
    def _():
        ch = 128
        lane = lax.broadcasted_iota(jnp.int32, (ch, 128), 1)
        lo = lane < HEAD_DIM
        top = lax.broadcasted_iota(jnp.int32, (128, ch), 0) < HEAD_DIM
        for c in range(seq // ch):
            rows = slice(c * ch, (c + 1) * ch)
            r = c * ch + lax.broadcasted_iota(jnp.int32, (ch, 128), 0)
            onehot = jnp.where((lane & (HEAD_DIM - 1)) == r // NSA_SLC_L, 1.0, 0.0)
            kf = ks_ref[rows, :].astype(F32)
            kaug[0, rows, :] = jnp.where(lo, kf, onehot).astype(BF16)
            kaug[1, rows, :] = jnp.where(lo, pltpu.roll(kf, HEAD_DIM, 1), onehot).astype(BF16)
            vt = vs_ref[rows, :].astype(F32).T
            cols = slice((c * ch) % kc, (c * ch) % kc + ch)
            vaugt[0, (c * ch) // kc, :, cols] = jnp.where(top, vt, 1.0).astype(BF16)
            vaugt[1, (c * ch) // kc, :, cols] = jnp.where(top, pltpu.roll(vt, HEAD_DIM, 0), 1.0).astype(BF16)

    qf = q_ref[...].astype(F32) * (HEAD_DIM ** -0.5)
    ws = []
    for hk in range(2):
        qt = qf[:, 128 * hk:128 * hk + 128].T.astype(BF16)
        bt = sb_ref[hk]
        ws.append(jnp.concatenate([jnp.concatenate([qt[0:HEAD_DIM], bt], axis=0),
                                   jnp.concatenate([qt[HEAD_DIM:128], bt], axis=0)], axis=1))

    m_s[...] = jnp.full(m_s.shape, NEG, F32)
    acc_s[...] = jnp.zeros(acc_s.shape, F32)

    rel = (lax.broadcasted_iota(jnp.int32, (kc, 2 * qb), 0)
           - (lax.broadcasted_iota(jnp.int32, (kc, 2 * qb), 1) & (qb - 1)))

    def scores(c, masked):
        out = []
        for hk in range(2):
            st = _dot(kaug[hk, c * kc:(c + 1) * kc, :], ws[hk])
            if masked:
                st = jnp.where(rel <= q0 - c * kc, st, NEG)
            out.append((st, jnp.max(st, axis=0, keepdims=True)))
        return out

    def consume(c, sm):
        m_old = [m_s[hk, 0:1, :] for hk in range(2)]
        m_new = [jnp.maximum(m_old[hk], sm[hk][1]) for hk in range(2)]
        pts = [jnp.exp(sm[hk][0] - m_new[hk]).astype(BF16) for hk in range(2)]
        pvs = [_dot(vaugt[hk, c], pts[hk]) for hk in range(2)]
        for hk in range(2):
            acc_s[hk] = acc_s[hk] * jnp.exp(m_old[hk] - m_new[hk]) + pvs[hk]
            m_s[hk] = jnp.broadcast_to(m_new[hk], (8, 2 * qb))

    n_last = (q0 + qb - 1) // kc

    def run(n_chunks):
        sm = scores(0, n_chunks == 1)
        for c in range(n_chunks):
            nxt = scores(c + 1, c + 2 == n_chunks) if c + 1 < n_chunks else None
            consume(c, sm)
            sm = nxt

    for n in range(seq // kc):
        pl.when(n_last == n)(functools.partial(run, n + 1))

    for hk in range(2):
        a = acc_s[hk]
        on = a[0:HEAD_DIM] * (1.0 / a[HEAD_DIM:128])
        stk = jnp.concatenate([on[:, 0:qb], on[:, qb:2 * qb]], axis=0)
        o_ref[:, 128 * hk:128 * hk + 128] = stk.T.astype(BF16)


def _slc_call(u3, sb, bsz, seq):
    nslc = seq // NSA_SLC_L
    return pl.pallas_call(
        functools.partial(_slc_body, seq=seq),
        out_shape=jax.ShapeDtypeStruct((bsz, seq, GROUP_W), BF16),
        grid=(bsz, seq // SLC_QB),
        in_specs=[
            pl.BlockSpec((None, SLC_QB, GROUP_W), lambda b, i: (b, i, COL['b_q'] // GROUP_W)),
            pl.BlockSpec((None, 2, nslc, SLC_QB), lambda b, i: (b, 0, 0, i)),
            pl.BlockSpec((None, seq, 128), lambda b, i: (b, 0, COL['b_ks'] // 128)),
            pl.BlockSpec((None, seq, 128), lambda b, i: (b, 0, COL['b_vs'] // 128)),
        ],
        out_specs=pl.BlockSpec((None, SLC_QB, GROUP_W), lambda b, i: (b, i, 0)),
        scratch_shapes=[pltpu.VMEM((2, seq, 128), BF16), pltpu.VMEM((2, seq // SLC_KC, 128, SLC_KC), BF16),
                        pltpu.VMEM((2, 8, 2 * SLC_QB), F32), pltpu.VMEM((2, 128, 2 * SLC_QB), F32)],
        compiler_params=_params(2),
        name="nsa_slc",
    )(u3, sb, u3, u3).reshape(bsz * seq, GROUP_W)


GLA_TC = 256
_GLA_NLEV = 6


def _gla_constants():
    c = GLA_CHUNK
    idx = np.arange(c)
    t = idx[None, :]
    mats = [(t <= idx[:, None])]
    masks = []
    for lev in range(_GLA_NLEV):
        m = (c // 2) >> lev
        blk = idx // m
        ref_q = blk * m
        ref_k = (blk + 1) * m
        odd = (blk % 2 == 1)
        dq = (t > ref_q[:, None]) & (t <= idx[:, None]) & odd[:, None]
        dk = (t > idx[:, None]) & (t <= ref_k[:, None]) & (~odd)[:, None]
        mats.append(dq | dk)
        same_parent = (idx[:, None] // (2 * m)) == (idx[None, :] // (2 * m))
        masks.append(same_parent & odd[:, None] & (~odd)[None, :])
    masks.append(np.eye(c, dtype=bool))
    dall = np.concatenate(mats, axis=0).astype(np.float32)
    mask4 = np.stack([np.tile(mk.astype(np.float32), (1, N_HEADS)) for mk in masks])
    hv = np.arange(GROUP_W) // HEAD_DIM
    hd = np.arange(N_HEADS * GLA_DK) // GLA_DK
    bd = (hv[:, None] == hd[None, :]).astype(np.float32)
    same_head = (hv[:, None] == hv[None, :]).astype(np.float32)
    return dall, mask4, bd, same_head


def _gla_body(cq_ref, ck_ref, cv_ref, misc_ref, wa_ref, ba_ref, dall_ref, mask_ref, bd_ref,
              sh_ref, gn_ref, o_ref, st):
    j = pl.program_id(1)

    @pl.when(j == 0)
    def _():
        st[...] = jnp.zeros_like(st)

    c = GLA_CHUNK
    nk = N_HEADS * GLA_DK
    lane_h = lax.broadcasted_iota(jnp.int32, (1, nk), 1) // GLA_DK
    lane_v = lax.broadcasted_iota(jnp.int32, (1, GROUP_W), 1) // HEAD_DIM
    dall = dall_ref[...]
    nch = GLA_TC // c
    rows = [slice(ch * c, (ch + 1) * c) for ch in range(nch)]
    z = _dot(misc_ref[...], wa_ref[...]) + ba_ref[...]
    la = (jnp.minimum(z, 0.0) - jnp.log1p(jnp.exp(-jnp.abs(z)))) * (1.0 / GLA_TAU)
    hi, lo = _split_hi_lo(la)
    hl = jnp.concatenate([hi, lo], axis=1)
    e_all = []
    for r in rows:
        e2 = _dot(dall, hl[r])
        e_all.append(e2[:, 0:nk] + e2[:, nk:2 * nk])
    qf = [cq_ref[r, :].astype(F32) * (GLA_DK ** -0.5) for r in rows]
    kf = [ck_ref[r, :].astype(F32) for r in rows]
    vs = [cv_ref[r, :] for r in rows]
    atts, q_in, kv_new, decay = [], [], [], []
    for ch in range(nch):
        bcum = e_all[ch][0:c]
        att = jnp.zeros((c, N_HEADS * c), F32)
        for lev in range(_GLA_NLEV + 1):
            if lev < _GLA_NLEV:
                ex = jnp.exp(e_all[ch][(1 + lev) * c:(2 + lev) * c])
                ql, kl = qf[ch] * ex, kf[ch] * ex
            else:
                ql, kl = qf[ch], kf[ch]
            klb = kl.astype(BF16)
            kbd = jnp.concatenate([jnp.where(lane_h == h, klb, jnp.zeros_like(klb))
                                   for h in range(N_HEADS)], axis=0)
            att = att + _dot_nt(ql.astype(BF16), kbd) * mask_ref[lev]
        atts.append(att.astype(BF16))
        q_in.append((qf[ch] * jnp.exp(bcum)).astype(BF16))
        b_last = bcum[c - 1:c, :]
        kt = (kf[ch] * jnp.exp(b_last - bcum)).astype(BF16)
        kv_new.append(bd_ref[...] * _dot_tn(vs[ch], kt))
        decay.append(jnp.exp(b_last))
    intra = []
    for ch in range(nch):
        vbd = jnp.concatenate([jnp.where(lane_v == h, vs[ch], jnp.zeros_like(vs[ch]))
                               for h in range(N_HEADS)], axis=0)
        intra.append(_dot(atts[ch], vbd))
    state = st[...]
    outs = []
    for ch in range(nch):
        outs.append(intra[ch] + _dot_nt(q_in[ch], state.astype(BF16)))
        state = state * decay[ch] + kv_new[ch]
    st[...] = state
    for ch in range(nch):
        o = outs[ch]
        ms = _dot_hl(o * o, sh_ref[...]) * (1.0 / HEAD_DIM)
        o_ref[rows[ch], :] = (o * lax.rsqrt(ms + EPS) * gn_ref[...]).astype(BF16)


def _gla_call(u3, wa_emb, ba, gnorm, bsz, seq):
    dall, mask4, bd, same_head = _gla_constants()
    tc = GLA_TC
    full = lambda shape: pl.BlockSpec(shape, lambda b, j: (0,) * len(shape))
    return pl.pallas_call(
        _gla_body,
        out_shape=jax.ShapeDtypeStruct((bsz, seq, GROUP_W), BF16),
        grid=(bsz, seq // tc),
        in_specs=[
            pl.BlockSpec((None, tc, 128), lambda b, j: (b, j, COL['c_q'] // 128)),
            pl.BlockSpec((None, tc, 128), lambda b, j: (b, j, COL['c_k'] // 128)),
            pl.BlockSpec((None, tc, GROUP_W), lambda b, j: (b, j, COL['c_v'] // GROUP_W)),
            pl.BlockSpec((None, tc, 128), lambda b, j: (b, j, COL['misc'] // 128)),
            full((128, 128)), full((1, 128)), full(dall.shape), full(mask4.shape),
            full(bd.shape), full(same_head.shape), full((1, GROUP_W)),
        ],
        out_specs=pl.BlockSpec((None, tc, GROUP_W), lambda b, j: (b, j, 0)),
        scratch_shapes=[pltpu.VMEM((GROUP_W, N_HEADS * GLA_DK), F32)],
        compiler_params=_params(2),
        name="gla",
    )(u3, u3, u3, u3, wa_emb, ba, jnp.asarray(dall, BF16), jnp.asarray(mask4, F32),
      jnp.asarray(bd, F32), jnp.asarray(same_head, BF16), gnorm).reshape(bsz * seq, GROUP_W)


OUT_TM = 512


def _out_body(x_ref, mod_ref, ln_ref, z_ref, o1_ref, o2_ref, o3_ref, l1_ref, l2_ref, l3_ref,
              ocmp_ref, oslc_ref, owin_ref, oc_ref, od_ref, misc_ref, gb_ref, e4_ref, eg_ref,
              w_ref, out_ref):
    l1, l2, l3 = l1_ref[...], l2_ref[...], l3_ref[...]
    mx = jnp.maximum(jnp.maximum(l1, l2), l3)
    e1, e2, e3 = jnp.exp(l1 - mx), jnp.exp(l2 - mx), jnp.exp(l3 - mx)
    inv = 1.0 / (e1 + e2 + e3)
    e4 = e4_ref[...]
    oa = jnp.zeros((OUT_TM, GROUP_W), F32)
    for e, o_ref in ((e1, o1_ref), (e2, o2_ref), (e3, o3_ref)):
        oa = oa + _dot_hl2(e * inv, e4) * o_ref[...].astype(F32)
    sg = jax.nn.sigmoid(misc_ref[...].astype(F32) + gb_ref[...])
    sg_hi, sg_lo = _split_hi_lo(sg)
    sg2 = jnp.concatenate([sg_hi, sg_lo], axis=1)
    ob = jnp.zeros((OUT_TM, GROUP_W), F32)
    for br, o_ref in enumerate((ocmp_ref, oslc_ref, owin_ref)):
        ob = ob + _dot(sg2, eg_ref[br]) * o_ref[...].astype(F32)
    z = z_ref[...].astype(F32)
    sz = z * jax.nn.sigmoid(z)
    mixed = jnp.concatenate([oa, ob, oc_ref[...].astype(F32), od_ref[...].astype(F32)], axis=1) * sz
    y = _dot(mixed.astype(BF16), w_ref[...])
    ms = jnp.mean(y * y, axis=-1, keepdims=True)
    yn = y * lax.rsqrt(ms + EPS) * ln_ref[...]
    out_ref[...] = x_ref[...] + mod_ref[2:3, :] * yn


def _out_call(x2, mod3, ln_post, u2, o1, o2, o3, l1, l2, l3, ocmp, oslc, owin, oc, od,
              gate_b_row, e4, eg, w_out, seq):
    t, d = x2.shape
    tm = OUT_TM
    per_b = seq // tm
    row = lambda w: pl.BlockSpec((tm, w), lambda i: (i, 0))
    full = lambda shape: pl.BlockSpec(shape, lambda i: (0,) * len(shape))
    return pl.pallas_call(
        _out_body,
        out_shape=jax.ShapeDtypeStruct((t, d), F32),
        grid=(t // tm,),
        in_specs=[
            row(d),
            pl.BlockSpec((None, 3, d), lambda i: (i // per_b, 0, 0)),
            full((1, d)),
            pl.BlockSpec((tm, 1024), lambda i: (i, COL['z'] // 1024)),
            row(GROUP_W), row(GROUP_W), row(GROUP_W), row(128), row(128), row(128),
            row(GROUP_W), row(GROUP_W), row(GROUP_W), row(GROUP_W), row(GROUP_W),
            pl.BlockSpec((tm, 128), lambda i: (i, COL['misc'] // 128)),
            full((1, 128)), full((256, GROUP_W)), full((3, 256, GROUP_W)), full((d, d)),
        ],
        out_specs=row(d),
        compiler_params=_params(1),
        name="mix_out",
    )(x2, mod3, ln_post.reshape(1, d), u2, o1, o2, o3, l1, l2, l3, ocmp, oslc, owin, oc, od,
      u2, gate_b_row, e4, eg, w_out)


def _expand_w1(w1):
    w = w1.reshape(2, NSA_CMP_D, HEAD_DIM, NSA_CMP_HID)
    zero = jnp.zeros((NSA_CMP_D, HEAD_DIM, NSA_CMP_HID), w1.dtype)
    top = jnp.concatenate([w[0], w[1], zero, zero], axis=-1)
    bot = jnp.concatenate([zero, zero, w[0], w[1]], axis=-1)
    return jnp.concatenate([top, bot], axis=1).astype(BF16)


def _expand_w2(w2):
    z = jnp.zeros_like(w2)
    h0 = jnp.concatenate([w2, w2, z, z], axis=1)
    h1 = jnp.concatenate([z, z, w2, w2], axis=1)
    return jnp.stack([h0, h1]).astype(BF16)


def _static_tables(seq):
    ng = seq // NSA_CMP_D
    nslc = seq // NSA_SLC_L
    cs = np.arange(ng) * NSA_CMP_D
    ss = np.arange(nslc) * NSA_SLC_L
    ovt = ((cs[None, :] < ss[:, None] + NSA_SLC_L) & (cs[None, :] + NSA_CMP_L > ss[:, None]))
    ovt[:, ng - 1] = False
    e4 = np.zeros((128, GROUP_W), np.float32)
    eg = np.zeros((3, 128, GROUP_W), np.float32)
    for h in range(N_HEADS):
        e4[32 * h, HEAD_DIM * h:HEAD_DIM * (h + 1)] = 1.0
        for br in range(3):
            eg[br, MISC_G_OFF + 3 * h + br, HEAD_DIM * h:HEAD_DIM * (h + 1)] = 1.0
    e4 = np.concatenate([e4, e4], axis=0)
    eg = np.concatenate([eg, eg], axis=1)
    return jnp.asarray(ovt, BF16), jnp.asarray(e4, BF16), jnp.asarray(eg, BF16)


def kernel(x, c, ln_pre, ln_post, w_ada, b_ada, w_in, w_out, nsa_pos_k, nsa_pos_v, nsa_w1_k, nsa_b1_k, nsa_w2_k, nsa_w1_v, nsa_b1_v, nsa_w2_v, nsa_gate_b, gla_w_a2, gla_b_a, gla_norm, sinks):
    bsz, seq, d = x.shape
    depth = w_in.shape[0]
    t = bsz * seq

    src = jnp.asarray(np.maximum(_SRC_COLS, 0))
    keep = jnp.asarray((_SRC_COLS >= 0).astype(np.float32))
    w_in_r = (jnp.take(w_in, src, axis=2) * keep).astype(BF16)
    w_out_b = w_out.astype(BF16)
    ovt, e4, eg = _static_tables(seq)

    mod = _ada_call(c, w_ada, b_ada).reshape(depth, bsz, 3, d)

    x2 = x.reshape(t, d)
    for i in range(depth):
        u2, ua, uc = _proj_in_call(x2, mod[i], ln_pre[i], w_in_r[i], seq)
        u3 = u2.reshape(bsz, seq, U_W)

        dil = [_dil_pair_call(ua, bsz, seq, w, r) for w, r in DIL_PAIRS]

        kc_rep, vc_rep = _compress_call(
            uc, bsz, seq, _expand_w1(nsa_w1_k[i]), _expand_w1(nsa_w1_v[i]),
            jnp.broadcast_to(nsa_pos_k[i].reshape(1, -1), (8, NSA_CMP_L * HEAD_DIM)).astype(BF16),
            jnp.broadcast_to(nsa_pos_v[i].reshape(1, -1), (8, NSA_CMP_L * HEAD_DIM)).astype(BF16),
            nsa_w1_k[i].astype(BF16), nsa_w1_v[i].astype(BF16),
            nsa_b1_k[i].reshape(1, -1), nsa_b1_v[i].reshape(1, -1),
            _expand_w2(nsa_w2_k[i]), _expand_w2(nsa_w2_v[i]))
        ocmp, sb = _cmp_sel_call(u3, kc_rep, vc_rep, ovt, bsz, seq)
        oslc = _slc_call(u3, sb, bsz, seq)
        owin = _win_call(u3, bsz, seq)

        wa_emb = jnp.zeros((128, 128), F32).at[MISC_A_OFF:MISC_A_OFF + GLA_RANK, :].set(gla_w_a2[i]).astype(BF16)
        oc = _gla_call(u3, wa_emb, gla_b_a[i].reshape(1, -1), gla_norm[i].reshape(1, -1), bsz, seq)

        od = _swa_call(u3, sinks[i], bsz, seq)

        gate_b_row = jnp.zeros((1, 128), F32).at[0, MISC_G_OFF:MISC_G_OFF + 3 * N_HEADS].set(nsa_gate_b[i])
        x2 = _out_call(x2, mod[i], ln_post[i], u2, dil[0][0], dil[1][0], dil[2][0],
                       dil[0][1], dil[1][1], dil[2][1], ocmp.reshape(t, GROUP_W), oslc, owin, oc, od,
                       gate_b_row, e4, eg, w_out_b[i], seq)
    return x2.reshape(bsz, seq, d)
```

```python
import functools

import numpy as np
import jax
import jax.numpy as jnp
from jax import lax
from jax.experimental import pallas as pl
from jax.experimental.pallas import tpu as pltpu

F32 = jnp.float32
BF16 = jnp.bfloat16

D_MODEL = 1024
HEAD_DIM = 64
N_HEADS = 4
GROUP_W = N_HEADS * HEAD_DIM
EPS = 1e-6
NEG = -1e30
QBLK = 128

DIL_PAIRS = ((128, 1), (512, 4), (2048, 16))
NSA_CMP_L = 32
NSA_CMP_D = 16
NSA_CMP_HID = 128
NSA_SLC_L = 64
NSA_TOP_N = 16
NSA_WIN = 512
GLA_DK = 32
GLA_RANK = 16
GLA_TAU = 16.0
GLA_CHUNK = 64
SWA_WIN = 128

VMEM_LIMIT = 56 * 1024 * 1024

_ORIG_SPLITS = (
    ('a_q', 256), ('a_k', 256), ('a_v', 256), ('a_z', 256),
    ('b_q', 256), ('b_kc', 128), ('b_vc', 128), ('b_ks', 128), ('b_vs', 128),
    ('b_kw', 128), ('b_vw', 128), ('b_g', 12), ('b_z', 256),
    ('c_q', 128), ('c_k', 128), ('c_v', 256), ('c_a', 16), ('c_z', 256),
    ('d_q', 256), ('d_k', 64), ('d_v', 64), ('d_z', 256),
)
_NEW_ORDER = (
    ('a_q', ('a_q',), 256), ('a_k', ('a_k',), 256), ('a_v', ('a_v',), 256),
    ('b_kc', ('b_kc',), 128), ('b_vc', ('b_vc',), 128),
    ('z', ('a_z', 'b_z', 'c_z', 'd_z'), 1024),
    ('b_q', ('b_q',), 256), ('d_q', ('d_q',), 256), ('c_v', ('c_v',), 256),
    ('b_ks', ('b_ks',), 128), ('b_vs', ('b_vs',), 128), ('b_kw', ('b_kw',), 128),
    ('b_vw', ('b_vw',), 128), ('c_q', ('c_q',), 128), ('c_k', ('c_k',), 128),
    ('d_kv', ('d_k', 'd_v'), 128), ('misc', ('b_g', 'c_a'), 128),
)
A_W = 768
C_W = 256
MISC_G_OFF = 0
MISC_A_OFF = 12


def _build_layout():
    off, o = {}, 0
    for name, w in _ORIG_SPLITS:
        off[name] = (o, w)
        o += w
    src, col, n = [], {}, 0
    for name, parts, width in _NEW_ORDER:
        col[name] = n
        used = 0
        for p in parts:
            s, w = off[p]
            src.extend(range(s, s + w))
            used += w
        src.extend([-1] * (width - used))
        n += width
    return np.asarray(src, np.int32), col, n


_SRC_COLS, COL, U_W = _build_layout()
assert COL['a_q'] == 0 and COL['b_kc'] == A_W and COL['z'] == A_W + C_W


def _dot(a, b):
    return jnp.dot(a, b, preferred_element_type=F32)


def _dot_nt(a, b):
    return lax.dot_general(a, b, (((1,), (1,)), ((), ())), preferred_element_type=F32)


def _dot_tn(a, b):
    return lax.dot_general(a, b, (((0,), (0,)), ((), ())), preferred_element_type=F32)


def _split_hi_lo(x):
    hi = x.astype(BF16)
    lo = (x - hi.astype(F32)).astype(BF16)
    return hi, lo


def _dot_hl(x, w):
    hi, lo = _split_hi_lo(x)
    return _dot(hi, w) + _dot(lo, w)


def _dot_hl2(x, w2):
    hi, lo = _split_hi_lo(x)
    return _dot(jnp.concatenate([hi, lo], axis=1), w2)


def _params(n_axes):
    return pltpu.CompilerParams(dimension_semantics=("arbitrary",) * n_axes,
                                vmem_limit_bytes=VMEM_LIMIT)


def _ada_body(c_ref, w_ref, b_ref, o_ref):
    c = c_ref[...]
    sc = (c * jax.nn.sigmoid(c)).astype(BF16)
    o_ref[...] = _dot(sc, w_ref[...].astype(BF16)) + b_ref[...]


def _ada_call(c, w_ada, b_ada):
    depth, d, d3 = w_ada.shape
    b = c.shape[0]
    nb = d3 // d
    return pl.pallas_call(
        _ada_body,
        out_shape=jax.ShapeDtypeStruct((depth, b, d3), F32),
        grid=(depth, nb),
        in_specs=[
            pl.BlockSpec((b, d), lambda i, n: (0, 0)),
            pl.BlockSpec((None, d, d), lambda i, n: (i, 0, n)),
            pl.BlockSpec((None, 1, d), lambda i, n: (i, 0, n)),
        ],
        out_specs=pl.BlockSpec((None, b, d), lambda i, n: (i, 0, n)),
        compiler_params=_params(2),
        name="ada_mod",
    )(c, w_ada, b_ada.reshape(depth, 1, d3))


PROJ_TM = 512
_PROJ_CHUNKS = ((0, 768), (768, 1024), (1024, 1792), (1792, 2560), (2560, 3328), (3328, 3840))


def _scatter_residues(src, dst_ref, r, rows, width):
    for j in range(r):
        for s in range(width // 128):
            dst_ref[:, j * width + 128 * s:j * width + 128 * (s + 1)] = (
                src[s, pl.ds(j, rows, stride=r), :].astype(dst_ref.dtype))


def _proj_in_body(x_ref, mod_ref, ln_ref, w_ref, o_ref, oa_ref, oa4_ref, oa16_ref, oc16_ref, sa, sc):
    x = x_ref[...]
    ms = jnp.mean(x * x, axis=-1, keepdims=True)
    y = x * lax.rsqrt(ms + EPS) * ln_ref[...]
    shift = mod_ref[0:1, :]
    scale = mod_ref[1:2, :]
    h = (y * (1.0 + scale) + shift).astype(BF16)
    tm = PROJ_TM
    for n0, n1 in _PROJ_CHUNKS:
        res = _dot(h, w_ref[:, n0:n1])
        r = res.astype(BF16)
        o_ref[:, n0:n1] = r
        if (n0, n1) == (0, A_W):
            oa_ref[...] = r
            for s in range(A_W // 128):
                sa[s] = res[:, 128 * s:128 * (s + 1)]
            _scatter_residues(sa, oa4_ref, 4, tm // 4, A_W)
            _scatter_residues(sa, oa16_ref, 16, tm // 16, A_W)
        elif (n0, n1) == (A_W, A_W + C_W):
            for s in range(C_W // 128):
                sc[s] = res[:, 128 * s:128 * (s + 1)]
            _scatter_residues(sc, oc16_ref, NSA_CMP_D, tm // NSA_CMP_D, C_W)


def _proj_in_call(x2, mod3, ln_pre, w_r, bsz, seq):
    t, d = x2.shape
    tm = PROJ_TM
    per_b = seq // tm
    grouped = lambda r, w: pl.BlockSpec((None, tm // r, r * w), lambda i: (i // per_b, i % per_b, 0))
    return pl.pallas_call(
        _proj_in_body,
        out_shape=(jax.ShapeDtypeStruct((t, U_W), BF16),
                   jax.ShapeDtypeStruct((t, A_W), BF16),
                   jax.ShapeDtypeStruct((bsz, seq // 4, 4 * A_W), BF16),
                   jax.ShapeDtypeStruct((bsz, seq // 16, 16 * A_W), BF16),
                   jax.ShapeDtypeStruct((bsz, seq // NSA_CMP_D, NSA_CMP_D * C_W), BF16)),
        grid=(t // tm,),
        in_specs=[
            pl.BlockSpec((tm, d), lambda i: (i, 0)),
            pl.BlockSpec((None, 3, d), lambda i: (i // per_b, 0, 0)),
            pl.BlockSpec((1, d), lambda i: (0, 0)),
            pl.BlockSpec((d, U_W), lambda i: (0, 0)),
        ],
        out_specs=(pl.BlockSpec((tm, U_W), lambda i: (i, 0)),
                   pl.BlockSpec((tm, A_W), lambda i: (i, 0)),
                   grouped(4, A_W), grouped(16, A_W), grouped(NSA_CMP_D, C_W)),
        scratch_shapes=[pltpu.VMEM((A_W // 128, tm, 128), F32), pltpu.VMEM((C_W // 128, tm, 128), F32)],
        compiler_params=_params(1),
        name="proj_in",
    )(x2, mod3, ln_pre.reshape(1, d), w_r)


def _stack_heads(q):
    lane_h = lax.broadcasted_iota(jnp.int32, (1, GROUP_W), 1) // HEAD_DIM
    zero = jnp.zeros_like(q)
    return jnp.concatenate([jnp.where(lane_h == h, q, zero) for h in range(N_HEADS)], axis=0)


def _pick_heads(r):
    lane_h = lax.broadcasted_iota(jnp.int32, (1, GROUP_W), 1) // HEAD_DIM
    out = jnp.where(lane_h == 0, r[0], 0.0)
    for h in range(1, N_HEADS):
        out = jnp.where(lane_h == h, r[h], out)
    return out


def _band_bias_table(blk, span, nb, back):
    d = np.arange(nb + 1)[:, None, None] * blk + np.arange(blk)[None, :, None] - np.arange(span)[None, None, :]
    return jnp.asarray(np.where((d >= 0) & (d <= back), 0.0, NEG), F32)


def _banded_core(i, q_ref, k_src, v_src, bias_ref, o_ref, lse_ref, sink_ref, *, blk, nsub, span, nb, back):
    if sink_ref is not None:
        hidx = lax.broadcasted_iota(jnp.int32, (N_HEADS, 1, 1), 0)
        sink = jnp.zeros((N_HEADS, 1, 1), F32)
        for h in range(N_HEADS):
            sink = jnp.where(hidx == h, sink_ref[h], sink)
    lane_l = lax.broadcasted_iota(jnp.int32, (1, 128), 1) // 32

    def window(sb):
        ib = i * nsub + sb
        return ib, pl.multiple_of(jnp.maximum(ib - nb, 0) * blk, blk)

    def qk(sb):
        ib, start = window(sb)
        q = q_ref[sb * blk:(sb + 1) * blk, :] * jnp.asarray(HEAD_DIM ** -0.5, BF16)
        s = _dot_nt(_stack_heads(q), k_src[pl.ds(start, span), :]).reshape(N_HEADS, blk, span)
        return s + bias_ref[jnp.minimum(ib, nb)][None]

    s_next = qk(0)
    for sb in range(nsub):
        s = s_next
        s_next = qk(sb + 1) if sb + 1 < nsub else None
        vb = v_src[pl.ds(window(sb)[1], span), :]
        m = jnp.max(s, axis=-1, keepdims=True)
        if sink_ref is not None:
            m = jnp.maximum(m, sink)
        p = jnp.exp(s - m)
        l = jnp.sum(p, axis=-1, keepdims=True)
        den = l if sink_ref is None else l + jnp.exp(sink - m)
        r = _dot(p.reshape(N_HEADS * blk, span).astype(BF16), vb).reshape(N_HEADS, blk, GROUP_W)
        o_ref[sb * blk:(sb + 1) * blk, :] = _pick_heads(r * (1.0 / den)).astype(o_ref.dtype)
        if lse_ref is not None:
            lse = m + jnp.log(l)
            acc = jnp.zeros((blk, 128), F32)
            for h in range(N_HEADS):
                acc = jnp.where(lane_l == h, lse[h], acc)
            lse_ref[sb * blk:(sb + 1) * blk, :] = acc


def _dil_body(q_ref, k_ref, v_ref, bias_ref, o_ref, lse_ref, **kw):
    _banded_core(pl.program_id(2), q_ref, k_ref, v_ref, bias_ref, o_ref, lse_ref, None, **kw)


def _banded_geometry(ln, back):
    blk = min(QBLK, ln)
    nb = -(-back // blk)
    span = min((nb + 1) * blk, ln)
    nsub = 4 if ln % (4 * blk) == 0 else (2 if ln % (2 * blk) == 0 else 1)
    return blk, nb, span, nsub


def _dil_pair_call(uv, bsz, seq, w, r):
    ln = seq // r
    back = w // r
    blk, nb, span, nsub = _banded_geometry(ln, back)
    rows = blk * nsub
    cpb = A_W // GROUP_W
    o, lse = pl.pallas_call(
        functools.partial(_dil_body, blk=blk, nsub=nsub, span=span, nb=nb, back=back),
        out_shape=(jax.ShapeDtypeStruct((bsz, ln, r * GROUP_W), BF16),
                   jax.ShapeDtypeStruct((bsz, ln, r * 128), F32)),
        grid=(bsz, r, ln // rows),
        in_specs=[
            pl.BlockSpec((None, rows, GROUP_W), lambda b, j, i: (b, i, j * cpb)),
            pl.BlockSpec((None, ln, GROUP_W), lambda b, j, i: (b, 0, j * cpb + 1)),
            pl.BlockSpec((None, ln, GROUP_W), lambda b, j, i: (b, 0, j * cpb + 2)),
            pl.BlockSpec((nb + 1, blk, span), lambda b, j, i: (0, 0, 0)),
        ],
        out_specs=(pl.BlockSpec((None, rows, GROUP_W), lambda b, j, i: (b, i, j)),
                   pl.BlockSpec((None, rows, 128), lambda b, j, i: (b, i, j))),
        compiler_params=_params(3),
        name=f"dil_r{r}",
    )(uv, uv, uv, _band_bias_table(blk, span, nb, back))
    return o, lse


def _rep_prep(src_refs, dst_refs, seq, mode):
    ch = 512
    lane = lax.broadcasted_iota(jnp.int32, (ch, 128), 1)
    lo = lane < HEAD_DIM
    for c in range(seq // ch):
        rows = slice(c * ch, (c + 1) * ch)
        if mode == 'swa':
            x = src_refs[0][rows, :].astype(F32)
            sw = pltpu.roll(x, HEAD_DIM, 1)
            kk = jnp.where(lo, x, sw).astype(BF16)
            vv = jnp.where(lo, sw, x).astype(BF16)
            dst_refs[0][rows, 0:128] = kk
            dst_refs[0][rows, 128:256] = kk
            dst_refs[1][rows, 0:128] = vv
            dst_refs[1][rows, 128:256] = vv
        else:
            for s_ref, d_ref in zip(src_refs, dst_refs):
                x = s_ref[rows, :].astype(F32)
                sw = pltpu.roll(x, HEAD_DIM, 1)
                d_ref[rows, 0:128] = jnp.where(lo, x, sw).astype(BF16)
                d_ref[rows, 128:256] = jnp.where(lo, sw, x).astype(BF16)


def _swa_body(sink_ref, q_ref, kv_ref, bias_ref, o_ref, kr, vr, *, seq, **kw):
    i = pl.program_id(1)

    @pl.when(i == 0)
    def _():
        _rep_prep((kv_ref,), (kr, vr), seq, 'swa')

    _banded_core(i, q_ref, kr, vr, bias_ref, o_ref, None, sink_ref, **kw)


def _win_body(q_ref, kw_ref, vw_ref, bias_ref, o_ref, kr, vr, *, seq, **kw):
    i = pl.program_id(1)

    @pl.when(i == 0)
    def _():
        _rep_prep((kw_ref, vw_ref), (kr, vr), seq, 'win')

    _banded_core(i, q_ref, kr, vr, bias_ref, o_ref, None, None, **kw)


def _swa_call(u3, sinks, bsz, seq):
    back = SWA_WIN - 1
    blk, nb, span, nsub = _banded_geometry(seq, back)
    rows = blk * nsub
    return pl.pallas_call(
        functools.partial(_swa_body, seq=seq, blk=blk, nsub=nsub, span=span, nb=nb, back=back),
        out_shape=jax.ShapeDtypeStruct((bsz, seq, GROUP_W), BF16),
        grid=(bsz, seq // rows),
        in_specs=[
            pl.BlockSpec(memory_space=pltpu.SMEM),
            pl.BlockSpec((None, rows, GROUP_W), lambda b, i: (b, i, COL['d_q'] // GROUP_W)),
            pl.BlockSpec((None, seq, 128), lambda b, i: (b, 0, COL['d_kv'] // 128)),
            pl.BlockSpec((nb + 1, blk, span), lambda b, i: (0, 0, 0)),
        ],
        out_specs=pl.BlockSpec((None, rows, GROUP_W), lambda b, i: (b, i, 0)),
        scratch_shapes=[pltpu.VMEM((seq, GROUP_W), BF16), pltpu.VMEM((seq, GROUP_W), BF16)],
        compiler_params=_params(2),
        name="swa",
    )(sinks, u3, u3, _band_bias_table(blk, span, nb, back)).reshape(bsz * seq, GROUP_W)


def _win_call(u3, bsz, seq):
    back = NSA_WIN - 1
    blk, nb, span, nsub = _banded_geometry(seq, back)
    rows = blk * nsub
    return pl.pallas_call(
        functools.partial(_win_body, seq=seq, blk=blk, nsub=nsub, span=span, nb=nb, back=back),
        out_shape=jax.ShapeDtypeStruct((bsz, seq, GROUP_W), BF16),
        grid=(bsz, seq // rows),
        in_specs=[
            pl.BlockSpec((None, rows, GROUP_W), lambda b, i: (b, i, COL['b_q'] // GROUP_W)),
            pl.BlockSpec((None, seq, 128), lambda b, i: (b, 0, COL['b_kw'] // 128)),
            pl.BlockSpec((None, seq, 128), lambda b, i: (b, 0, COL['b_vw'] // 128)),
            pl.BlockSpec((nb + 1, blk, span), lambda b, i: (0, 0, 0)),
        ],
        out_specs=pl.BlockSpec((None, rows, GROUP_W), lambda b, i: (b, i, 0)),
        scratch_shapes=[pltpu.VMEM((seq, GROUP_W), BF16), pltpu.VMEM((seq, GROUP_W), BF16)],
        compiler_params=_params(2),
        name="nsa_win",
    )(u3, u3, u3, _band_bias_table(blk, span, nb, back)).reshape(bsz * seq, GROUP_W)


def _compress_body(x_ref, w_ref, pk_ref, pv_ref, w1k_ref, w1v_ref,
                   b1k_ref, b1v_ref, w2k_ref, w2v_ref, ok_ref, ov_ref, *, ng):
    acc_kv = _dot(x_ref[...], w_ref[...])
    for off, p_ref, w1_ref, b1_ref, w2_ref, o_ref in (
            (0, pk_ref, w1k_ref, b1k_ref, w2k_ref, ok_ref),
            (512, pv_ref, w1v_ref, b1v_ref, w2v_ref, ov_ref)):
        const = _dot(p_ref[...], w1_ref[...])[0:1, :] + b1_ref[...]
        out = jnp.zeros((ng, GROUP_W), F32)
        for h in range(2):
            first = acc_kv[:, off + 256 * h:off + 256 * h + 128]
            second = acc_kv[:, off + 256 * h + 128:off + 256 * h + 256]
            pre = first + pltpu.roll(second, ng - 1, 0) + const
            act = jax.nn.gelu(pre, approximate=True)
            out = out + _dot(act.astype(BF16), w2_ref[h])
        o_ref[...] = out.astype(BF16)


def _compress_call(uc16, bsz, seq, w_all, posk, posv, w1k, w1v, b1k, b1v, w2k_exp, w2v_exp):
    ng = seq // NSA_CMP_D
    full = lambda shape: pl.BlockSpec(shape, lambda b: (0,) * len(shape))
    return pl.pallas_call(
        functools.partial(_compress_body, ng=ng),
        out_shape=(jax.ShapeDtypeStruct((bsz, ng, GROUP_W), BF16),
                   jax.ShapeDtypeStruct((bsz, ng, GROUP_W), BF16)),
        grid=(bsz,),
        in_specs=[
            pl.BlockSpec((None, ng, NSA_CMP_D * C_W), lambda b: (b, 0, 0)),
            full((NSA_CMP_D * C_W, 1024)),
            full((8, NSA_CMP_L * HEAD_DIM)), full((8, NSA_CMP_L * HEAD_DIM)),
            full((NSA_CMP_L * HEAD_DIM, NSA_CMP_HID)), full((NSA_CMP_L * HEAD_DIM, NSA_CMP_HID)),
            full((1, NSA_CMP_HID)), full((1, NSA_CMP_HID)),
            full((2, NSA_CMP_HID, GROUP_W)), full((2, NSA_CMP_HID, GROUP_W)),
        ],
        out_specs=(pl.BlockSpec((None, ng, GROUP_W), lambda b: (b, 0, 0)),
                   pl.BlockSpec((None, ng, GROUP_W), lambda b: (b, 0, 0))),
        compiler_params=_params(1),
        name="nsa_compress",
    )(uc16, w_all, posk, posv, w1k, w1v, b1k, b1v, w2k_exp, w2v_exp)


CMP_NSB = 2


def _cmp_sel_body(q_ref, kc_ref, vc_ref, ovt_ref, o_ref, sb_ref, scr, *, ng, nslc):
    i = pl.program_id(1)
    q0 = i * (CMP_NSB * QBLK)
    ci = lax.broadcasted_iota(jnp.int32, (QBLK, ng), 1)
    trow = lax.broadcasted_iota(jnp.int32, (QBLK, ng), 0)
    jrow = lax.broadcasted_iota(jnp.int32, (nslc, QBLK), 0)
    qcol = lax.broadcasted_iota(jnp.int32, (nslc, QBLK), 1)
    ovt = ovt_ref[...]

    def qk(sb):
        q = q_ref[sb * QBLK:(sb + 1) * QBLK, :] * jnp.asarray(HEAD_DIM ** -0.5, BF16)
        return _dot_nt(_stack_heads(q), kc_ref[...]).reshape(N_HEADS, QBLK, ng)

    forced, causal = [], []
    s_next = qk(0)
    for sb in range(CMP_NSB):
        s_raw = s_next
        s_next = qk(sb + 1) if sb + 1 < CMP_NSB else None
        qs = q0 + sb * QBLK
        valid = ((ci * NSA_CMP_D + (NSA_CMP_L - 1)) <= qs + trow)[None]
        s = jnp.where(valid, s_raw, NEG)
        m = jnp.max(s, axis=-1, keepdims=True)
        e = jnp.where(valid, jnp.exp(s - m), 0.0)
        l = jnp.sum(e, axis=-1, keepdims=True)
        p = e * (1.0 / jnp.maximum(l, 1e-30))
        r = _dot(p.reshape(N_HEADS * QBLK, ng).astype(BF16), vc_ref[...]).reshape(N_HEADS, QBLK, GROUP_W)
        o_ref[sb * QBLK:(sb + 1) * QBLK, :] = _pick_heads(r).astype(BF16)

        cur = (qs + qcol) // NSA_SLC_L
        forced.append(jnp.where(jrow == 0, 1.0, jnp.where(jrow == cur, 1.0, jnp.where(jrow == cur - 1, 1.0, 0.0))))
        causal.append(jnp.where(jrow <= cur, 1.0, 0.0))
        for hk in range(2):
            psum = p[2 * hk] + p[2 * hk + 1]
            hi, lo = _split_hi_lo(psum)
            imp_t = _dot_nt(ovt, hi) + _dot_nt(ovt, lo)
            scr[2 * sb + hk] = jnp.where(causal[sb] * (1.0 - forced[sb]) > 0.5, imp_t, -1.0)

    n_free = NSA_TOP_N - 3
    nv = nslc // 8
    sub = lax.broadcasted_iota(jnp.int32, (8, QBLK), 0)
    tier = (q0 + CMP_NSB * QBLK - 1) // (8 * NSA_SLC_L)
    for tt in range(nv):
        @pl.when(tier == tt)
        def _(tt=tt):
            nvt = tt + 1
            for sb in range(CMP_NSB):
                for hk in range(2):
                    k = 2 * sb + hk
                    vals = [scr[k, 8 * v:8 * v + 8, :] for v in range(nvt)]
                    cnts = [jnp.zeros((8, QBLK), F32) for _ in range(nvt)]
                    for jp in range(8 * nvt):
                        rv = jnp.broadcast_to(scr[k, jp:jp + 1, :], (8, QBLK))
                        for v in range(nvt):
                            if 8 * v > jp:
                                beats = jnp.where(rv >= vals[v], 1.0, 0.0)
                            elif 8 * v + 7 <= jp:
                                beats = jnp.where(rv > vals[v], 1.0, 0.0)
                            else:
                                tie = jnp.where(sub + 8 * v > jp, jnp.where(rv == vals[v], 1.0, 0.0), 0.0)
                                beats = jnp.where(rv > vals[v], 1.0, 0.0) + tie
                            cnts[v] = cnts[v] + beats
                    pieces = [jnp.where(c < n_free - 0.5, 1.0, 0.0) for c in cnts]
                    if nvt < nv:
                        pieces.append(jnp.zeros((8 * (nv - nvt), QBLK), F32))
                    picked = jnp.concatenate(pieces, axis=0)
                    sel_t = causal[sb] * jnp.maximum(forced[sb], picked)
                    sb_ref[hk, :, sb * QBLK:(sb + 1) * QBLK] = jnp.where(sel_t > 0.5, 0.0, NEG).astype(BF16)


def _cmp_sel_call(u3, kc_rep, vc_rep, ovt, bsz, seq):
    ng = seq // NSA_CMP_D
    nslc = seq // NSA_SLC_L
    assert nslc == 64, "selection-bias layout assumes 64 selection blocks"
    qb = CMP_NSB * QBLK
    return pl.pallas_call(
        functools.partial(_cmp_sel_body, ng=ng, nslc=nslc),
        out_shape=(jax.ShapeDtypeStruct((bsz, seq, GROUP_W), BF16),
                   jax.ShapeDtypeStruct((bsz, 2, nslc, seq), BF16)),
        grid=(bsz, seq // qb),
        in_specs=[
            pl.BlockSpec((None, qb, GROUP_W), lambda b, i: (b, i, COL['b_q'] // GROUP_W)),
            pl.BlockSpec((None, ng, GROUP_W), lambda b, i: (b, 0, 0)),
            pl.BlockSpec((None, ng, GROUP_W), lambda b, i: (b, 0, 0)),
            pl.BlockSpec((nslc, ng), lambda b, i: (0, 0)),
        ],
        out_specs=(pl.BlockSpec((None, qb, GROUP_W), lambda b, i: (b, i, 0)),
                   pl.BlockSpec((None, 2, nslc, qb), lambda b, i: (b, 0, 0, i))),
        scratch_shapes=[pltpu.VMEM((2 * CMP_NSB, nslc, QBLK), F32)],
        compiler_params=_params(2),
        name="nsa_cmp_sel",
    )(u3, kc_rep, vc_rep, ovt)


SLC_KC = 512
SLC_QB = 256


def _slc_body(q_ref, sb_ref, ks_ref, vs_ref, o_ref, kaug, vaugt, m_s, acc_s, *, seq):
    i = pl.program_id(1)
    qb = SLC_QB
    q0 = i * qb
    kc = SLC_KC

    @pl.when(i == 0)
    def _():
        ch = 128
        lane = lax.broadcasted_iota(jnp.int32, (ch, 128), 1)
        lo = lane < HEAD_DIM
        top = lax.broadcasted_iota(jnp.int32, (128, ch), 0) < HEAD_DIM
        for c in range(seq // ch):
            rows = slice(c * ch, (c + 1) * ch)
            r = c * ch + lax.broadcasted_iota(jnp.int32, (ch, 128), 0)
            onehot = jnp.where((lane & (HEAD_DIM - 1)) == r // NSA_SLC_L, 1.0, 0.0)
            kf = ks_ref[rows, :].astype(F32)
            kaug[0, rows, :] = jnp.where(lo, kf, onehot).astype(BF16)
            kaug[1, rows, :] = jnp.where(lo, pltpu.roll(kf, HEAD_DIM, 1), onehot).astype(BF16)
            vt = vs_ref[rows, :].astype(F32).T
            cols = slice((c * ch) % kc, (c * ch) % kc + ch)
            vaugt[0, (c * ch) // kc, :, cols] = jnp.where(top, vt, 1.0).astype(BF16)
            vaugt[1, (c * ch) // kc, :, cols] = jnp.where(top, pltpu.roll(vt, HEAD_DIM, 0), 1.0).astype(BF16)

    qf = q_ref[...].astype(F32) * (HEAD_DIM ** -0.5)
    ws = []
    for hk in range(2):
        qt = qf[:, 128 * hk:128 * hk + 128].T.astype(BF16)
        bt = sb_ref[hk]
        ws.append(jnp.concatenate([jnp.concatenate([qt[0:HEAD_DIM], bt], axis=0),
                                   jnp.concatenate([qt[HEAD_DIM:128], bt], axis=0)], axis=1))

    m_s[...] = jnp.full(m_s.shape, NEG, F32)
    acc_s[...] = jnp.zeros(acc_s.shape, F32)

    rel = (lax.broadcasted_iota(jnp.int32, (kc, 2 * qb), 0)
           - (lax.broadcasted_iota(jnp.int32, (kc, 2 * qb), 1) & (qb - 1)))

    def scores(c, masked):
        out = []
        for hk in range(2):
            st = _dot(kaug[hk, c * kc:(c + 1) * kc, :], ws[hk])
            if masked:
                st = jnp.where(rel <= q0 - c * kc, st, NEG)
            out.append((st, jnp.max(st, axis=0, keepdims=True)))
        return out

    def consume(c, sm):
        m_old = [m_s[hk, 0:1, :] for hk in range(2)]
        m_new = [jnp.maximum(m_old[hk], sm[hk][1]) for hk in range(2)]
        pts = [jnp.exp(sm[hk][0] - m_new[hk]).astype(BF16) for hk in range(2)]
        pvs = [_dot(vaugt[hk, c], pts[hk]) for hk in range(2)]
        for hk in range(2):
            acc_s[hk] = acc_s[hk] * jnp.exp(m_old[hk] - m_new[hk]) + pvs[hk]
            m_s[hk] = jnp.broadcast_to(m_new[hk], (8, 2 * qb))

    n_last = (q0 + qb - 1) // kc

    def run(n_chunks):
        sm = scores(0, n_chunks == 1)
        for c in range(n_chunks):
            nxt = scores(c + 1, c + 2 == n_chunks) if c + 1 < n_chunks else None
            consume(c, sm)
            sm = nxt

    for n in range(seq // kc):
        pl.when(n_last == n)(functools.partial(run, n + 1))

    for hk in range(2):
        a = acc_s[hk]
        on = a[0:HEAD_DIM] * (1.0 / a[HEAD_DIM:128])
        stk = jnp.concatenate([on[:, 0:qb], on[:, qb:2 * qb]], axis=0)
        o_ref[:, 128 * hk:128 * hk + 128] = stk.T.astype(BF16)


def _slc_call(u3, sb, bsz, seq):
    nslc = seq // NSA_SLC_L
    return pl.pallas_call(
        functools.partial(_slc_body, seq=seq),
        out_shape=jax.ShapeDtypeStruct((bsz, seq, GROUP_W), BF16),
        grid=(bsz, seq // SLC_QB),
        in_specs=[
            pl.BlockSpec((None, SLC_QB, GROUP_W), lambda b, i: (b, i, COL['b_q'] // GROUP_W)),
            pl.BlockSpec((None, 2, nslc, SLC_QB), lambda b, i: (b, 0, 0, i)),
            pl.BlockSpec((None, seq, 128), lambda b, i: (b, 0, COL['b_ks'] // 128)),
            pl.BlockSpec((None, seq, 128), lambda b, i: (b, 0, COL['b_vs'] // 128)),
        ],
        out_specs=pl.BlockSpec((None, SLC_QB, GROUP_W), lambda b, i: (b, i, 0)),
        scratch_shapes=[pltpu.VMEM((2, seq, 128), BF16), pltpu.VMEM((2, seq // SLC_KC, 128, SLC_KC), BF16),
                        pltpu.VMEM((2, 8, 2 * SLC_QB), F32), pltpu.VMEM((2, 128, 2 * SLC_QB), F32)],
        compiler_params=_params(2),
        name="nsa_slc",
    )(u3, sb, u3, u3).reshape(bsz * seq, GROUP_W)


GLA_TC = 256
_GLA_NLEV = 6


def _gla_constants():
    c = GLA_CHUNK
    idx = np.arange(c)
    t = idx[None, :]
    mats = [(t <= idx[:, None])]
    masks = []
    for lev in range(_GLA_NLEV):
        m = (c // 2) >> lev
        blk = idx // m
        ref_q = blk * m
        ref_k = (blk + 1) * m
        odd = (blk % 2 == 1)
        dq = (t > ref_q[:, None]) & (t <= idx[:, None]) & odd[:, None]
        dk = (t > idx[:, None]) & (t <= ref_k[:, None]) & (~odd)[:, None]
        mats.append(dq | dk)
        same_parent = (idx[:, None] // (2 * m)) == (idx[None, :] // (2 * m))
        masks.append(same_parent & odd[:, None] & (~odd)[None, :])
    masks.append(np.eye(c, dtype=bool))
    dall = np.concatenate(mats, axis=0).astype(np.float32)
    mask4 = np.stack([np.tile(mk.astype(np.float32), (1, N_HEADS)) for mk in masks])
    hv = np.arange(GROUP_W) // HEAD_DIM
    hd = np.arange(N_HEADS * GLA_DK) // GLA_DK
    bd = (hv[:, None] == hd[None, :]).astype(np.float32)
    same_head = (hv[:, None] == hv[None, :]).astype(np.float32)
    return dall, mask4, bd, same_head


def _gla_body(cq_ref, ck_ref, cv_ref, misc_ref, wa_ref, ba_ref, dall_ref, mask_ref, bd_ref,
              sh_ref, gn_ref, o_ref, st):
    j = pl.program_id(1)

    @pl.when(j == 0)
    def _():
        st[...] = jnp.zeros_like(st)

    c = GLA_CHUNK
    nk = N_HEADS * GLA_DK
    lane_h = lax.broadcasted_iota(jnp.int32, (1, nk), 1) // GLA_DK
    lane_v = lax.broadcasted_iota(jnp.int32, (1, GROUP_W), 1) // HEAD_DIM
    dall = dall_ref[...]
    nch = GLA_TC // c
    rows = [slice(ch * c, (ch + 1) * c) for ch in range(nch)]
    z = _dot(misc_ref[...], wa_ref[...]) + ba_ref[...]
    la = (jnp.minimum(z, 0.0) - jnp.log1p(jnp.exp(-jnp.abs(z)))) * (1.0 / GLA_TAU)
    hi, lo = _split_hi_lo(la)
    hl = jnp.concatenate([hi, lo], axis=1)
    e_all = []
    for r in rows:
        e2 = _dot(dall, hl[r])
        e_all.append(e2[:, 0:nk] + e2[:, nk:2 * nk])
    qf = [cq_ref[r, :].astype(F32) * (GLA_DK ** -0.5) for r in rows]
    kf = [ck_ref[r, :].astype(F32) for r in rows]
    vs = [cv_ref[r, :] for r in rows]
    atts, q_in, kv_new, decay = [], [], [], []
    for ch in range(nch):
        bcum = e_all[ch][0:c]
        att = jnp.zeros((c, N_HEADS * c), F32)
        for lev in range(_GLA_NLEV + 1):
            if lev < _GLA_NLEV:
                ex = jnp.exp(e_all[ch][(1 + lev) * c:(2 + lev) * c])
                ql, kl = qf[ch] * ex, kf[ch] * ex
            else:
                ql, kl = qf[ch], kf[ch]
            klb = kl.astype(BF16)
            kbd = jnp.concatenate([jnp.where(lane_h == h, klb, jnp.zeros_like(klb))
                                   for h in range(N_HEADS)], axis=0)
            att = att + _dot_nt(ql.astype(BF16), kbd) * mask_ref[lev]
        atts.append(att.astype(BF16))
        q_in.append((qf[ch] * jnp.exp(bcum)).astype(BF16))
        b_last = bcum[c - 1:c, :]
        kt = (kf[ch] * jnp.exp(b_last - bcum)).astype(BF16)
        kv_new.append(bd_ref[...] * _dot_tn(vs[ch], kt))
        decay.append(jnp.exp(b_last))
    intra = []
    for ch in range(nch):
        vbd = jnp.concatenate([jnp.where(lane_v == h, vs[ch], jnp.zeros_like(vs[ch]))
                               for h in range(N_HEADS)], axis=0)
        intra.append(_dot(atts[ch], vbd))
    state = st[...]
    outs = []
    for ch in range(nch):
        outs.append(intra[ch] + _dot_nt(q_in[ch], state.astype(BF16)))
        state = state * decay[ch] + kv_new[ch]
    st[...] = state
    for ch in range(nch):
        o = outs[ch]
        ms = _dot_hl(o * o, sh_ref[...]) * (1.0 / HEAD_DIM)
        o_ref[rows[ch], :] = (o * lax.rsqrt(ms + EPS) * gn_ref[...]).astype(BF16)


def _gla_call(u3, wa_emb, ba, gnorm, bsz, seq):
    dall, mask4, bd, same_head = _gla_constants()
    tc = GLA_TC
    full = lambda shape: pl.BlockSpec(shape, lambda b, j: (0,) * len(shape))
    return pl.pallas_call(
        _gla_body,
        out_shape=jax.ShapeDtypeStruct((bsz, seq, GROUP_W), BF16),
        grid=(bsz, seq // tc),
        in_specs=[
            pl.BlockSpec((None, tc, 128), lambda b, j: (b, j, COL['c_q'] // 128)),
            pl.BlockSpec((None, tc, 128), lambda b, j: (b, j, COL['c_k'] // 128)),
            pl.BlockSpec((None, tc, GROUP_W), lambda b, j: (b, j, COL['c_v'] // GROUP_W)),
            pl.BlockSpec((None, tc, 128), lambda b, j: (b, j, COL['misc'] // 128)),
            full((128, 128)), full((1, 128)), full(dall.shape), full(mask4.shape),
            full(bd.shape), full(same_head.shape), full((1, GROUP_W)),
        ],
        out_specs=pl.BlockSpec((None, tc, GROUP_W), lambda b, j: (b, j, 0)),
        scratch_shapes=[pltpu.VMEM((GROUP_W, N_HEADS * GLA_DK), F32)],
        compiler_params=_params(2),
        name="gla",
    )(u3, u3, u3, u3, wa_emb, ba, jnp.asarray(dall, BF16), jnp.asarray(mask4, F32),
      jnp.asarray(bd, F32), jnp.asarray(same_head, BF16), gnorm).reshape(bsz * seq, GROUP_W)


OUT_TM = 512


def _gather_residues(src_ref, dst, r, rows, width):
    for j in range(r):
        for s in range(width // 128):
            dst[s, pl.ds(j, rows, stride=r), :] = (
                src_ref[:, j * width + 128 * s:j * width + 128 * (s + 1)].astype(F32))


def _out_body(x_ref, mod_ref, ln_ref, z_ref, o1_ref, o2_ref, o3_ref, l1_ref, l2_ref, l3_ref,
              ocmp_ref, oslc_ref, owin_ref, oc_ref, od_ref, misc_ref, gb_ref, e4_ref, eg_ref,
              w_ref, out_ref, so2, sl2, so3, sl3):
    tm = OUT_TM
    _gather_residues(o2_ref, so2, 4, tm // 4, GROUP_W)
    _gather_residues(l2_ref, sl2, 4, tm // 4, 128)
    _gather_residues(o3_ref, so3, 16, tm // 16, GROUP_W)
    _gather_residues(l3_ref, sl3, 16, tm // 16, 128)
    l1, l2, l3 = l1_ref[...], sl2[0], sl3[0]
    mx = jnp.maximum(jnp.maximum(l1, l2), l3)
    e1, e2, e3 = jnp.exp(l1 - mx), jnp.exp(l2 - mx), jnp.exp(l3 - mx)
    inv = 1.0 / (e1 + e2 + e3)
    e4 = e4_ref[...]
    pair_out = (o1_ref[...].astype(F32),
                jnp.concatenate([so2[0], so2[1]], axis=1),
                jnp.concatenate([so3[0], so3[1]], axis=1))
    oa = jnp.zeros((OUT_TM, GROUP_W), F32)
    for e, o in zip((e1, e2, e3), pair_out):
        oa = oa + _dot_hl2(e * inv, e4) * o
    sg = jax.nn.sigmoid(misc_ref[...].astype(F32) + gb_ref[...])
    sg_hi, sg_lo = _split_hi_lo(sg)
    sg2 = jnp.concatenate([sg_hi, sg_lo], axis=1)
    ob = jnp.zeros((OUT_TM, GROUP_W), F32)
    for br, o_ref in enumerate((ocmp_ref, oslc_ref, owin_ref)):
        ob = ob + _dot(sg2, eg_ref[br]) * o_ref[...].astype(F32)
    z = z_ref[...].astype(F32)
    sz = z * jax.nn.sigmoid(z)
    mixed = jnp.concatenate([oa, ob, oc_ref[...].astype(F32), od_ref[...].astype(F32)], axis=1) * sz
    y = _dot(mixed.astype(BF16), w_ref[...])
    ms = jnp.mean(y * y, axis=-1, keepdims=True)
    yn = y * lax.rsqrt(ms + EPS) * ln_ref[...]
    out_ref[...] = x_ref[...] + mod_ref[2:3, :] * yn


def _out_call(x2, mod3, ln_post, u2, o1, o2, o3, l1, l2, l3, ocmp, oslc, owin, oc, od,
              gate_b_row, e4, eg, w_out, seq):
    t, d = x2.shape
    tm = OUT_TM
    per_b = seq // tm
    row = lambda w: pl.BlockSpec((tm, w), lambda i: (i, 0))
    full = lambda shape: pl.BlockSpec(shape, lambda i: (0,) * len(shape))
    grouped = lambda r, w: pl.BlockSpec((None, tm // r, r * w), lambda i: (i // per_b, i % per_b, 0))
    return pl.pallas_call(
        _out_body,
        out_shape=jax.ShapeDtypeStruct((t, d), F32),
        grid=(t // tm,),
        in_specs=[
            row(d),
            pl.BlockSpec((None, 3, d), lambda i: (i // per_b, 0, 0)),
            full((1, d)),
            pl.BlockSpec((tm, 1024), lambda i: (i, COL['z'] // 1024)),
            row(GROUP_W), grouped(4, GROUP_W), grouped(16, GROUP_W),
            row(128), grouped(4, 128), grouped(16, 128),
            row(GROUP_W), row(GROUP_W), row(GROUP_W), row(GROUP_W), row(GROUP_W),
            pl.BlockSpec((tm, 128), lambda i: (i, COL['misc'] // 128)),
            full((1, 128)), full((256, GROUP_W)), full((3, 256, GROUP_W)), full((d, d)),
        ],
        out_specs=row(d),
        scratch_shapes=[pltpu.VMEM((2, tm, 128), F32), pltpu.VMEM((1, tm, 128), F32),
                        pltpu.VMEM((2, tm, 128), F32), pltpu.VMEM((1, tm, 128), F32)],
        compiler_params=_params(1),
        name="mix_out",
    )(x2, mod3, ln_post.reshape(1, d), u2, o1, o2, o3, l1, l2, l3, ocmp, oslc, owin, oc, od,
      u2, gate_b_row, e4, eg, w_out)


def _expand_w1(w1):
    w = w1.reshape(2, NSA_CMP_D, HEAD_DIM, NSA_CMP_HID)
    zero = jnp.zeros((NSA_CMP_D, HEAD_DIM, NSA_CMP_HID), w1.dtype)
    top = jnp.concatenate([w[0], w[1], zero, zero], axis=-1)
    bot = jnp.concatenate([zero, zero, w[0], w[1]], axis=-1)
    return jnp.concatenate([top, bot], axis=1).astype(BF16)


def _compress_weights(w1k, w1v):
    wk, wv = _expand_w1(w1k), _expand_w1(w1v)
    zero = jnp.zeros_like(wk)
    rows_k = jnp.concatenate([wk, zero], axis=-1)
    rows_v = jnp.concatenate([zero, wv], axis=-1)
    return jnp.concatenate([rows_k, rows_v], axis=1).reshape(NSA_CMP_D * C_W, 1024)


def _expand_w2(w2):
    z = jnp.zeros_like(w2)
    h0 = jnp.concatenate([w2, w2, z, z], axis=1)
    h1 = jnp.concatenate([z, z, w2, w2], axis=1)
    return jnp.stack([h0, h1]).astype(BF16)


def _static_tables(seq):
    ng = seq // NSA_CMP_D
    nslc = seq // NSA_SLC_L
    cs = np.arange(ng) * NSA_CMP_D
    ss = np.arange(nslc) * NSA_SLC_L
    ovt = ((cs[None, :] < ss[:, None] + NSA_SLC_L) & (cs[None, :] + NSA_CMP_L > ss[:, None]))
    ovt[:, ng - 1] = False
    e4 = np.zeros((128, GROUP_W), np.float32)
    eg = np.zeros((3, 128, GROUP_W), np.float32)
    for h in range(N_HEADS):
        e4[32 * h, HEAD_DIM * h:HEAD_DIM * (h + 1)] = 1.0
        for br in range(3):
            eg[br, MISC_G_OFF + 3 * h + br, HEAD_DIM * h:HEAD_DIM * (h + 1)] = 1.0
    e4 = np.concatenate([e4, e4], axis=0)
    eg = np.concatenate([eg, eg], axis=1)
    return jnp.asarray(ovt, BF16), jnp.asarray(e4, BF16), jnp.asarray(eg, BF16)


def kernel(x, c, ln_pre, ln_post, w_ada, b_ada, w_in, w_out, nsa_pos_k, nsa_pos_v, nsa_w1_k, nsa_b1_k, nsa_w2_k, nsa_w1_v, nsa_b1_v, nsa_w2_v, nsa_gate_b, gla_w_a2, gla_b_a, gla_norm, sinks):
    bsz, seq, d = x.shape
    depth = w_in.shape[0]
    t = bsz * seq

    src = jnp.asarray(np.maximum(_SRC_COLS, 0))
    keep = jnp.asarray((_SRC_COLS >= 0).astype(np.float32))
    w_in_r = (jnp.take(w_in, src, axis=2) * keep).astype(BF16)
    w_out_b = w_out.astype(BF16)
    ovt, e4, eg = _static_tables(seq)

    mod = _ada_call(c, w_ada, b_ada).reshape(depth, bsz, 3, d)

    x2 = x.reshape(t, d)
    for i in range(depth):
        u2, ua, ua4, ua16, uc16 = _proj_in_call(x2, mod[i], ln_pre[i], w_in_r[i], bsz, seq)
        u3 = u2.reshape(bsz, seq, U_W)

        grouped_a = {1: ua.reshape(bsz, seq, A_W), 4: ua4, 16: ua16}
        dil = [_dil_pair_call(grouped_a[r], bsz, seq, w, r) for w, r in DIL_PAIRS]

        kc_rep, vc_rep = _compress_call(
            uc16, bsz, seq, _compress_weights(nsa_w1_k[i], nsa_w1_v[i]),
            jnp.broadcast_to(nsa_pos_k[i].reshape(1, -1), (8, NSA_CMP_L * HEAD_DIM)).astype(BF16),
            jnp.broadcast_to(nsa_pos_v[i].reshape(1, -1), (8, NSA_CMP_L * HEAD_DIM)).astype(BF16),
            nsa_w1_k[i].astype(BF16), nsa_w1_v[i].astype(BF16),
            nsa_b1_k[i].reshape(1, -1), nsa_b1_v[i].reshape(1, -1),
            _expand_w2(nsa_w2_k[i]), _expand_w2(nsa_w2_v[i]))
        ocmp, sb = _cmp_sel_call(u3, kc_rep, vc_rep, ovt, bsz, seq)
        oslc = _slc_call(u3, sb, bsz, seq)
        owin = _win_call(u3, bsz, seq)

        wa_emb = jnp.zeros((128, 128), F32).at[MISC_A_OFF:MISC_A_OFF + GLA_RANK, :].set(gla_w_a2[i]).astype(BF16)
        oc = _gla_call(u3, wa_emb, gla_b_a[i].reshape(1, -1), gla_norm[i].reshape(1, -1), bsz, seq)

        od = _swa_call(u3, sinks[i], bsz, seq)

        gate_b_row = jnp.zeros((1, 128), F32).at[0, MISC_G_OFF:MISC_G_OFF + 3 * N_HEADS].set(nsa_gate_b[i])
        x2 = _out_call(x2, mod[i], ln_post[i], u2, dil[0][0].reshape(t, GROUP_W), dil[1][0], dil[2][0],
                       dil[0][1].reshape(t, 128), dil[1][1], dil[2][1], ocmp.reshape(t, GROUP_W), oslc, owin, oc, od,
                       gate_b_row, e4, eg, w_out_b[i], seq)
    return x2.reshape(bsz, seq, d)
```

```python
import functools

import numpy as np
import jax
import jax.numpy as jnp
from jax import lax
from jax.experimental import pallas as pl
from jax.experimental.pallas import tpu as pltpu

F32 = jnp.float32
BF16 = jnp.bfloat16

D_MODEL = 1024
HEAD_DIM = 64
N_HEADS = 4
GROUP_W = N_HEADS * HEAD_DIM
EPS = 1e-6
NEG = -1e30
QBLK = 128

DIL_PAIRS = ((128, 1), (512, 4), (2048, 16))
NSA_CMP_L = 32
NSA_CMP_D = 16
NSA_CMP_HID = 128
NSA_SLC_L = 64
NSA_TOP_N = 16
NSA_WIN = 512
GLA_DK = 32
GLA_RANK = 16
GLA_TAU = 16.0
GLA_CHUNK = 64
SWA_WIN = 128

VMEM_LIMIT = 56 * 1024 * 1024

_ORIG_SPLITS = (
    ('a_q', 256), ('a_k', 256), ('a_v', 256), ('a_z', 256),
    ('b_q', 256), ('b_kc', 128), ('b_vc', 128), ('b_ks', 128), ('b_vs', 128),
    ('b_kw', 128), ('b_vw', 128), ('b_g', 12), ('b_z', 256),
    ('c_q', 128), ('c_k', 128), ('c_v', 256), ('c_a', 16), ('c_z', 256),
    ('d_q', 256), ('d_k', 64), ('d_v', 64), ('d_z', 256),
)
_NEW_ORDER = (
    ('a_q', ('a_q',), 256), ('a_k', ('a_k',), 256), ('a_v', ('a_v',), 256),
    ('b_kc', ('b_kc',), 128), ('b_vc', ('b_vc',), 128),
    ('z', ('a_z', 'b_z', 'c_z', 'd_z'), 1024),
    ('b_q', ('b_q',), 256), ('d_q', ('d_q',), 256), ('c_v', ('c_v',), 256),
    ('b_ks', ('b_ks',), 128), ('b_vs', ('b_vs',), 128), ('b_kw', ('b_kw',), 128),
    ('b_vw', ('b_vw',), 128), ('c_q', ('c_q',), 128), ('c_k', ('c_k',), 128),
    ('d_kv', ('d_k', 'd_v'), 128), ('misc', ('b_g', 'c_a'), 128),
)
A_W = 768
C_W = 256
MISC_G_OFF = 0
MISC_A_OFF = 12


def _build_layout():
    off, o = {}, 0
    for name, w in _ORIG_SPLITS:
        off[name] = (o, w)
        o += w
    src, col, n = [], {}, 0
    for name, parts, width in _NEW_ORDER:
        col[name] = n
        used = 0
        for p in parts:
            s, w = off[p]
            src.extend(range(s, s + w))
            used += w
        src.extend([-1] * (width - used))
        n += width
    return np.asarray(src, np.int32), col, n


_SRC_COLS, COL, U_W = _build_layout()
assert COL['a_q'] == 0 and COL['b_kc'] == A_W and COL['z'] == A_W + C_W


def _dot(a, b):
    return jnp.dot(a, b, preferred_element_type=F32)


def _dot_nt(a, b):
    return lax.dot_general(a, b, (((1,), (1,)), ((), ())), preferred_element_type=F32)


def _dot_tn(a, b):
    return lax.dot_general(a, b, (((0,), (0,)), ((), ())), preferred_element_type=F32)


def _split_hi_lo(x):
    hi = x.astype(BF16)
    lo = (x - hi.astype(F32)).astype(BF16)
    return hi, lo


def _dot_hl(x, w):
    hi, lo = _split_hi_lo(x)
    return _dot(hi, w) + _dot(lo, w)


def _dot_hl2(x, w2):
    hi, lo = _split_hi_lo(x)
    return _dot(jnp.concatenate([hi, lo], axis=1), w2)


def _params(n_axes):
    return pltpu.CompilerParams(dimension_semantics=("arbitrary",) * n_axes,
                                vmem_limit_bytes=VMEM_LIMIT)


def _ada_body(c_ref, w_ref, b_ref, o_ref):
    c = c_ref[...]
    sc = (c * jax.nn.sigmoid(c)).astype(BF16)
    o_ref[...] = _dot(sc, w_ref[...].astype(BF16)) + b_ref[...]


def _ada_call(c, w_ada, b_ada):
    depth, d, d3 = w_ada.shape
    b = c.shape[0]
    nb = d3 // d
    return pl.pallas_call(
        _ada_body,
        out_shape=jax.ShapeDtypeStruct((depth, b, d3), F32),
        grid=(depth, nb),
        in_specs=[
            pl.BlockSpec((b, d), lambda i, n: (0, 0)),
            pl.BlockSpec((None, d, d), lambda i, n: (i, 0, n)),
            pl.BlockSpec((None, 1, d), lambda i, n: (i, 0, n)),
        ],
        out_specs=pl.BlockSpec((None, b, d), lambda i, n: (i, 0, n)),
        compiler_params=_params(2),
        name="ada_mod",
    )(c, w_ada, b_ada.reshape(depth, 1, d3))


PROJ_TM = 512
_PROJ_CHUNKS = ((0, 768), (768, 1024), (1024, 1792), (1792, 2560), (2560, 3328), (3328, 3840))


def _scatter_4_then_16(src, dst4_ref, dst16_ref, mid, rows4, width):
    ns = width // 128
    for j4 in range(4):
        for s in range(ns):
            v = src[s, pl.ds(j4, rows4, stride=4), :]
            if dst4_ref is not None:
                dst4_ref[:, j4 * width + 128 * s:j4 * width + 128 * (s + 1)] = v.astype(dst4_ref.dtype)
            mid[j4 * ns + s] = v
    for j4 in range(4):
        for e in range(4):
            for s in range(ns):
                c0 = (4 * e + j4) * width + 128 * s
                dst16_ref[:, c0:c0 + 128] = mid[j4 * ns + s, pl.ds(e, rows4 // 4, stride=4), :].astype(dst16_ref.dtype)


def _proj_in_body(x_ref, mod_ref, ln_ref, w_ref, o_ref, oa_ref, oa4_ref, oa16_ref, oc16_ref, sa, sc, mida, midc):
    x = x_ref[...]
    ms = jnp.mean(x * x, axis=-1, keepdims=True)
    y = x * lax.rsqrt(ms + EPS) * ln_ref[...]
    shift = mod_ref[0:1, :]
    scale = mod_ref[1:2, :]
    h = (y * (1.0 + scale) + shift).astype(BF16)
    tm = PROJ_TM
    for n0, n1 in _PROJ_CHUNKS:
        res = _dot(h, w_ref[:, n0:n1])
        r = res.astype(BF16)
        o_ref[:, n0:n1] = r
        if (n0, n1) == (0, A_W):
            oa_ref[...] = r
            for s in range(A_W // 128):
                sa[s] = res[:, 128 * s:128 * (s + 1)]
            _scatter_4_then_16(sa, oa4_ref, oa16_ref, mida, tm // 4, A_W)
        elif (n0, n1) == (A_W, A_W + C_W):
            for s in range(C_W // 128):
                sc[s] = res[:, 128 * s:128 * (s + 1)]
            _scatter_4_then_16(sc, None, oc16_ref, midc, tm // 4, C_W)


def _proj_in_call(x2, mod3, ln_pre, w_r, bsz, seq):
    t, d = x2.shape
    tm = PROJ_TM
    per_b = seq // tm
    grouped = lambda r, w: pl.BlockSpec((None, tm // r, r * w), lambda i: (i // per_b, i % per_b, 0))
    return pl.pallas_call(
        _proj_in_body,
        out_shape=(jax.ShapeDtypeStruct((t, U_W), BF16),
                   jax.ShapeDtypeStruct((t, A_W), BF16),
                   jax.ShapeDtypeStruct((bsz, seq // 4, 4 * A_W), BF16),
                   jax.ShapeDtypeStruct((bsz, seq // 16, 16 * A_W), BF16),
                   jax.ShapeDtypeStruct((bsz, seq // NSA_CMP_D, NSA_CMP_D * C_W), BF16)),
        grid=(t // tm,),
        in_specs=[
            pl.BlockSpec((tm, d), lambda i: (i, 0)),
            pl.BlockSpec((None, 3, d), lambda i: (i // per_b, 0, 0)),
            pl.BlockSpec((1, d), lambda i: (0, 0)),
            pl.BlockSpec((d, U_W), lambda i: (0, 0)),
        ],
        out_specs=(pl.BlockSpec((tm, U_W), lambda i: (i, 0)),
                   pl.BlockSpec((tm, A_W), lambda i: (i, 0)),
                   grouped(4, A_W), grouped(16, A_W), grouped(NSA_CMP_D, C_W)),
        scratch_shapes=[pltpu.VMEM((A_W // 128, tm, 128), F32), pltpu.VMEM((C_W // 128, tm, 128), F32),
                        pltpu.VMEM((4 * A_W // 128, tm // 4, 128), F32),
                        pltpu.VMEM((4 * C_W // 128, tm // 4, 128), F32)],
        compiler_params=_params(1),
        name="proj_in",
    )(x2, mod3, ln_pre.reshape(1, d), w_r)


def _stack_heads(q):
    lane_h = lax.broadcasted_iota(jnp.int32, (1, GROUP_W), 1) // HEAD_DIM
    zero = jnp.zeros_like(q)
    return jnp.concatenate([jnp.where(lane_h == h, q, zero) for h in range(N_HEADS)], axis=0)


def _pick_heads(r):
    lane_h = lax.broadcasted_iota(jnp.int32, (1, GROUP_W), 1) // HEAD_DIM
    out = jnp.where(lane_h == 0, r[0], 0.0)
    for h in range(1, N_HEADS):
        out = jnp.where(lane_h == h, r[h], out)
    return out


def _band_bias_table(blk, span, nb, back):
    d = np.arange(nb + 1)[:, None, None] * blk + np.arange(blk)[None, :, None] - np.arange(span)[None, None, :]
    return jnp.asarray(np.where((d >= 0) & (d <= back), 0.0, NEG), F32)


def _banded_core(i, q_ref, k_src, v_src, bias_ref, o_ref, lse_ref, *, blk, nsub, span, nb, back):
    lane_l = lax.broadcasted_iota(jnp.int32, (1, 128), 1) // 32

    def window(sb):
        ib = i * nsub + sb
        return ib, pl.multiple_of(jnp.maximum(ib - nb, 0) * blk, blk)

    def qk(sb):
        ib, start = window(sb)
        q = q_ref[sb * blk:(sb + 1) * blk, :] * jnp.asarray(HEAD_DIM ** -0.5, BF16)
        s = _dot_nt(_stack_heads(q), k_src[pl.ds(start, span), :]).reshape(N_HEADS, blk, span)
        return s + bias_ref[jnp.minimum(ib, nb)][None]

    s_next = qk(0)
    for sb in range(nsub):
        s = s_next
        s_next = qk(sb + 1) if sb + 1 < nsub else None
        vb = v_src[pl.ds(window(sb)[1], span), :]
        m = jnp.max(s, axis=-1, keepdims=True)
        p = jnp.exp(s - m)
        l = jnp.sum(p, axis=-1, keepdims=True)
        r = _dot(p.reshape(N_HEADS * blk, span).astype(BF16), vb).reshape(N_HEADS, blk, GROUP_W)
        o_ref[sb * blk:(sb + 1) * blk, :] = _pick_heads(r * (1.0 / l)).astype(o_ref.dtype)
        lse = m + jnp.log(l)
        acc = jnp.zeros((blk, 128), F32)
        for h in range(N_HEADS):
            acc = jnp.where(lane_l == h, lse[h], acc)
        lse_ref[sb * blk:(sb + 1) * blk, :] = acc


def _dil_body(q_ref, k_ref, v_ref, bias_ref, o_ref, lse_ref, **kw):
    _banded_core(pl.program_id(2), q_ref, k_ref, v_ref, bias_ref, o_ref, lse_ref, **kw)


def _dil_multi_body(qkv_ref, bias_ref, o_ref, lse_ref, *, nres, **kw):
    for jj in range(nres):
        c0 = jj * A_W
        _banded_core(0, qkv_ref.at[:, c0:c0 + GROUP_W], qkv_ref.at[:, c0 + GROUP_W:c0 + 2 * GROUP_W],
                     qkv_ref.at[:, c0 + 2 * GROUP_W:c0 + 3 * GROUP_W], bias_ref,
                     o_ref.at[:, jj * GROUP_W:(jj + 1) * GROUP_W], lse_ref.at[:, jj * 128:(jj + 1) * 128], **kw)


DIL_RES_PER_STEP = 4


def _banded_geometry(ln, back):
    blk = min(QBLK, ln)
    nb = -(-back // blk)
    span = min((nb + 1) * blk, ln)
    nsub = 4 if ln % (4 * blk) == 0 else (2 if ln % (2 * blk) == 0 else 1)
    return blk, nb, span, nsub


def _dil_pair_call(uv, bsz, seq, w, r):
    ln = seq // r
    back = w // r
    blk, nb, span, nsub = _banded_geometry(ln, back)
    rows = blk * nsub
    cpb = A_W // GROUP_W
    nres = DIL_RES_PER_STEP
    if rows == ln and r % nres == 0:
        return pl.pallas_call(
            functools.partial(_dil_multi_body, nres=nres, blk=blk, nsub=nsub, span=span, nb=nb, back=back),
            out_shape=(jax.ShapeDtypeStruct((bsz, ln, r * GROUP_W), BF16),
                       jax.ShapeDtypeStruct((bsz, ln, r * 128), F32)),
            grid=(bsz, r // nres),
            in_specs=[
                pl.BlockSpec((None, ln, nres * A_W), lambda b, j: (b, 0, j)),
                pl.BlockSpec((nb + 1, blk, span), lambda b, j: (0, 0, 0)),
            ],
            out_specs=(pl.BlockSpec((None, ln, nres * GROUP_W), lambda b, j: (b, 0, j)),
                       pl.BlockSpec((None, ln, nres * 128), lambda b, j: (b, 0, j))),
            compiler_params=_params(2),
            name=f"dil_r{r}",
        )(uv, _band_bias_table(blk, span, nb, back))
    o, lse = pl.pallas_call(
        functools.partial(_dil_body, blk=blk, nsub=nsub, span=span, nb=nb, back=back),
        out_shape=(jax.ShapeDtypeStruct((bsz, ln, r * GROUP_W), BF16),
                   jax.ShapeDtypeStruct((bsz, ln, r * 128), F32)),
        grid=(bsz, r, ln // rows),
        in_specs=[
            pl.BlockSpec((None, rows, GROUP_W), lambda b, j, i: (b, i, j * cpb)),
            pl.BlockSpec((None, ln, GROUP_W), lambda b, j, i: (b, 0, j * cpb + 1)),
            pl.BlockSpec((None, ln, GROUP_W), lambda b, j, i: (b, 0, j * cpb + 2)),
            pl.BlockSpec((nb + 1, blk, span), lambda b, j, i: (0, 0, 0)),
        ],
        out_specs=(pl.BlockSpec((None, rows, GROUP_W), lambda b, j, i: (b, i, j)),
                   pl.BlockSpec((None, rows, 128), lambda b, j, i: (b, i, j))),
        compiler_params=_params(3),
        name=f"dil_r{r}",
    )(uv, uv, uv, _band_bias_table(blk, span, nb, back))
    return o, lse


def _gqa_bias_tables(kc, qb, rels, back, n_rep):
    key = np.arange(kc)[None, :, None] + np.asarray(rels)[:, None, None] * kc
    d = np.arange(qb)[None, None, :] - key
    t = np.where((d >= 0) & (d <= back), 0.0, NEG).astype(np.float32)
    return jnp.asarray(np.tile(t, (1, 1, n_rep)))


def _gqa_body(*refs, seq, kc, qb, nsb, rels, groups, has_sink):
    if has_sink:
        sink_ref, refs = refs[0], refs[1:]
    q_ref, k_ref, v_ref, bias_ref, o_ref, vaugt = refs
    i = pl.program_id(1)

    @pl.when(i == 0)
    def _():
        ch = 128
        top = lax.broadcasted_iota(jnp.int32, (128, ch), 0) < HEAD_DIM
        for c in range(seq // ch):
            vt = v_ref[c * ch:(c + 1) * ch, :].astype(F32).T
            cols = slice((c * ch) % kc, (c * ch) % kc + ch)
            for g, (_, _, _, v_off) in enumerate(groups):
                vg = vt if v_off == 0 else pltpu.roll(vt, HEAD_DIM, 0)
                vaugt[g, (c * ch) // kc, :, cols] = jnp.where(top, vg, 1.0).astype(BF16)

    zero = jnp.zeros((HEAD_DIM, qb), BF16)

    def scores(sb):
        q0 = (i * nsb + sb) * qb
        qf = q_ref[sb * qb:(sb + 1) * qb, :].astype(F32) * (HEAD_DIM ** -0.5)
        qt = [qf[:, 0:128].T.astype(BF16), qf[:, 128:256].T.astype(BF16)]
        head_t = [qt[h // 2][(h % 2) * HEAD_DIM:(h % 2 + 1) * HEAD_DIM] for h in range(N_HEADS)]
        offs = [q0 + rel * kc for rel in rels]
        starts = [pl.multiple_of(jnp.maximum(o, 0), kc) for o in offs]
        sts = []
        for h0, nh, k_off, _ in groups:
            w = jnp.concatenate(
                [jnp.concatenate([head_t[h], zero] if k_off == 0 else [zero, head_t[h]], axis=0)
                 for h in range(h0, h0 + nh)], axis=1)
            per_chunk = []
            for c, rel in enumerate(rels):
                st = _dot(k_ref[pl.ds(starts[c], kc), :], w) + bias_ref[c]
                if rel < 0:
                    st = st + jnp.where(offs[c] >= 0, 0.0, NEG)
                per_chunk.append(st)
            sts.append(per_chunk)
        return sts, starts

    def finish(sb, sts, starts):
        outs = []
        for g, (h0, nh, _, _) in enumerate(groups):
            m = functools.reduce(jnp.maximum, [jnp.max(st, axis=0, keepdims=True) for st in sts[g]])
            if has_sink:
                lane_hd = lax.broadcasted_iota(jnp.int32, (1, nh * qb), 1) // qb
                sink = jnp.zeros((1, nh * qb), F32)
                for hh in range(nh):
                    sink = jnp.where(lane_hd == hh, sink_ref[h0 + hh], sink)
                m = jnp.maximum(m, sink)
            acc = None
            for c in range(len(rels)):
                p = jnp.exp(sts[g][c] - m).astype(BF16)
                pv = _dot(vaugt[g, starts[c] // kc], p)
                acc = pv if acc is None else acc + pv
            den = acc[HEAD_DIM:128]
            if has_sink:
                den = den + jnp.exp(sink - m)
            on = acc[0:HEAD_DIM] * (1.0 / den)
            outs.extend(on[:, hh * qb:(hh + 1) * qb] for hh in range(nh))
        for j in range(N_HEADS // 2):
            stk = jnp.concatenate([outs[2 * j], outs[2 * j + 1]], axis=0)
            o_ref[sb * qb:(sb + 1) * qb, 128 * j:128 * (j + 1)] = stk.T.astype(o_ref.dtype)

    nxt = scores(0)
    for sb in range(nsb):
        cur = nxt
        nxt = scores(sb + 1) if sb + 1 < nsb else None
        finish(sb, *cur)


def _gqa_call(name, sinks, u3, bsz, seq, q_col, k_col, v_col, groups, back, kc, qb, nsb):
    rels = tuple(range(-(-(-back // kc)), qb // kc))
    nh_max = max(g[1] for g in groups)
    bias = _gqa_bias_tables(kc, qb, rels, back, nh_max)
    rows = qb * nsb
    in_specs = [
        pl.BlockSpec((None, rows, GROUP_W), lambda b, i: (b, i, q_col // GROUP_W)),
        pl.BlockSpec((None, seq, 128), lambda b, i: (b, 0, k_col // 128)),
        pl.BlockSpec((None, seq, 128), lambda b, i: (b, 0, v_col // 128)),
        pl.BlockSpec(bias.shape, lambda b, i: (0, 0, 0)),
    ]
    args = (u3, u3, u3, bias)
    if sinks is not None:
        in_specs = [pl.BlockSpec(memory_space=pltpu.SMEM)] + in_specs
        args = (sinks,) + args
    return pl.pallas_call(
        functools.partial(_gqa_body, seq=seq, kc=kc, qb=qb, nsb=nsb, rels=rels, groups=groups,
                          has_sink=sinks is not None),
        out_shape=jax.ShapeDtypeStruct((bsz, seq, GROUP_W), BF16),
        grid=(bsz, seq // rows),
        in_specs=in_specs,
        out_specs=pl.BlockSpec((None, rows, GROUP_W), lambda b, i: (b, i, 0)),
        scratch_shapes=[pltpu.VMEM((len(groups), seq // kc, 128, kc), BF16)],
        compiler_params=_params(2),
        name=name,
    )(*args).reshape(bsz * seq, GROUP_W)


def _swa_call(u3, sinks, bsz, seq):
    return _gqa_call("swa", sinks, u3, bsz, seq, COL['d_q'], COL['d_kv'], COL['d_kv'],
                     ((0, N_HEADS, 0, HEAD_DIM),), SWA_WIN - 1, kc=128, qb=128, nsb=4)


def _win_call(u3, bsz, seq):
    return _gqa_call("nsa_win", None, u3, bsz, seq, COL['b_q'], COL['b_kw'], COL['b_vw'],
                     ((0, 2, 0, 0), (2, 2, HEAD_DIM, HEAD_DIM)), NSA_WIN - 1, kc=256, qb=256, nsb=2)


def _compress_body(x_ref, w_ref, pk_ref, pv_ref, w1k_ref, w1v_ref,
                   b1k_ref, b1v_ref, w2k_ref, w2v_ref, ok_ref, ov_ref, *, ng):
    acc_kv = _dot(x_ref[...], w_ref[...])
    for off, p_ref, w1_ref, b1_ref, w2_ref, o_ref in (
            (0, pk_ref, w1k_ref, b1k_ref, w2k_ref, ok_ref),
            (512, pv_ref, w1v_ref, b1v_ref, w2v_ref, ov_ref)):
        const = _dot(p_ref[...], w1_ref[...])[0:1, :] + b1_ref[...]
        out = jnp.zeros((ng, GROUP_W), F32)
        for h in range(2):
            first = acc_kv[:, off + 256 * h:off + 256 * h + 128]
            second = acc_kv[:, off + 256 * h + 128:off + 256 * h + 256]
            pre = first + pltpu.roll(second, ng - 1, 0) + const
            act = jax.nn.gelu(pre, approximate=True)
            out = out + _dot(act.astype(BF16), w2_ref[h])
        o_ref[...] = out.astype(BF16)


def _compress_call(uc16, bsz, seq, w_all, posk, posv, w1k, w1v, b1k, b1v, w2k_exp, w2v_exp):
    ng = seq // NSA_CMP_D
    full = lambda shape: pl.BlockSpec(shape, lambda b: (0,) * len(shape))
    return pl.pallas_call(
        functools.partial(_compress_body, ng=ng),
        out_shape=(jax.ShapeDtypeStruct((bsz, ng, GROUP_W), BF16),
                   jax.ShapeDtypeStruct((bsz, ng, GROUP_W), BF16)),
        grid=(bsz,),
        in_specs=[
            pl.BlockSpec((None, ng, NSA_CMP_D * C_W), lambda b: (b, 0, 0)),
            full((NSA_CMP_D * C_W, 1024)),
            full((8, NSA_CMP_L * HEAD_DIM)), full((8, NSA_CMP_L * HEAD_DIM)),
            full((NSA_CMP_L * HEAD_DIM, NSA_CMP_HID)), full((NSA_CMP_L * HEAD_DIM, NSA_CMP_HID)),
            full((1, NSA_CMP_HID)), full((1, NSA_CMP_HID)),
            full((2, NSA_CMP_HID, GROUP_W)), full((2, NSA_CMP_HID, GROUP_W)),
        ],
        out_specs=(pl.BlockSpec((None, ng, GROUP_W), lambda b: (b, 0, 0)),
                   pl.BlockSpec((None, ng, GROUP_W), lambda b: (b, 0, 0))),
        compiler_params=_params(1),
        name="nsa_compress",
    )(uc16, w_all, posk, posv, w1k, w1v, b1k, b1v, w2k_exp, w2v_exp)


CMP_NSB = 2


def _cmp_sel_body(q_ref, kc_ref, vc_ref, ovt_ref, o_ref, sb_ref, scr, *, ng, nslc):
    i = pl.program_id(1)
    q0 = i * (CMP_NSB * QBLK)
    ci = lax.broadcasted_iota(jnp.int32, (QBLK, ng), 1)
    trow = lax.broadcasted_iota(jnp.int32, (QBLK, ng), 0)
    jrow = lax.broadcasted_iota(jnp.int32, (nslc, QBLK), 0)
    qcol = lax.broadcasted_iota(jnp.int32, (nslc, QBLK), 1)
    ovt = ovt_ref[...]

    def qk(sb):
        q = q_ref[sb * QBLK:(sb + 1) * QBLK, :] * jnp.asarray(HEAD_DIM ** -0.5, BF16)
        return _dot_nt(_stack_heads(q), kc_ref[...]).reshape(N_HEADS, QBLK, ng)

    forced, causal = [], []
    s_next = qk(0)
    for sb in range(CMP_NSB):
        s_raw = s_next
        s_next = qk(sb + 1) if sb + 1 < CMP_NSB else None
        qs = q0 + sb * QBLK
        valid = ((ci * NSA_CMP_D + (NSA_CMP_L - 1)) <= qs + trow)[None]
        s = jnp.where(valid, s_raw, NEG)
        m = jnp.max(s, axis=-1, keepdims=True)
        e = jnp.where(valid, jnp.exp(s - m), 0.0)
        l = jnp.sum(e, axis=-1, keepdims=True)
        p = e * (1.0 / jnp.maximum(l, 1e-30))
        r = _dot(p.reshape(N_HEADS * QBLK, ng).astype(BF16), vc_ref[...]).reshape(N_HEADS, QBLK, GROUP_W)
        o_ref[sb * QBLK:(sb + 1) * QBLK, :] = _pick_heads(r).astype(BF16)

        cur = (qs + qcol) // NSA_SLC_L
        forced.append(jnp.where(jrow == 0, 1.0, jnp.where(jrow == cur, 1.0, jnp.where(jrow == cur - 1, 1.0, 0.0))))
        causal.append(jnp.where(jrow <= cur, 1.0, 0.0))
        for hk in range(2):
            psum = p[2 * hk] + p[2 * hk + 1]
            hi, lo = _split_hi_lo(psum)
            imp_t = _dot_nt(ovt, hi) + _dot_nt(ovt, lo)
            scr[2 * sb + hk] = jnp.where(causal[sb] * (1.0 - forced[sb]) > 0.5, imp_t, -1.0)

    n_free = NSA_TOP_N - 3
    nv = nslc // 8
    sub = lax.broadcasted_iota(jnp.int32, (8, QBLK), 0)
    tier = (q0 + CMP_NSB * QBLK - 1) // (8 * NSA_SLC_L)
    for tt in range(nv):
        @pl.when(tier == tt)
        def _(tt=tt):
            nvt = tt + 1
            for sb in range(CMP_NSB):
                for hk in range(2):
                    k = 2 * sb + hk
                    vals = [scr[k, 8 * v:8 * v + 8, :] for v in range(nvt)]
                    cnts = [jnp.zeros((8, QBLK), F32) for _ in range(nvt)]
                    for jp in range(8 * nvt):
                        rv = jnp.broadcast_to(scr[k, jp:jp + 1, :], (8, QBLK))
                        for v in range(nvt):
                            if 8 * v > jp:
                                beats = jnp.where(rv >= vals[v], 1.0, 0.0)
                            elif 8 * v + 7 <= jp:
                                beats = jnp.where(rv > vals[v], 1.0, 0.0)
                            else:
                                tie = jnp.where(sub + 8 * v > jp, jnp.where(rv == vals[v], 1.0, 0.0), 0.0)
                                beats = jnp.where(rv > vals[v], 1.0, 0.0) + tie
                            cnts[v] = cnts[v] + beats
                    pieces = [jnp.where(c < n_free - 0.5, 1.0, 0.0) for c in cnts]
                    if nvt < nv:
                        pieces.append(jnp.zeros((8 * (nv - nvt), QBLK), F32))
                    picked = jnp.concatenate(pieces, axis=0)
                    sel_t = causal[sb] * jnp.maximum(forced[sb], picked)
                    sb_ref[hk, :, sb * QBLK:(sb + 1) * QBLK] = jnp.where(sel_t > 0.5, 0.0, NEG).astype(BF16)


def _cmp_sel_call(u3, kc_rep, vc_rep, ovt, bsz, seq):
    ng = seq // NSA_CMP_D
    nslc = seq // NSA_SLC_L
    assert nslc == 64, "selection-bias layout assumes 64 selection blocks"
    qb = CMP_NSB * QBLK
    return pl.pallas_call(
        functools.partial(_cmp_sel_body, ng=ng, nslc=nslc),
        out_shape=(jax.ShapeDtypeStruct((bsz, seq, GROUP_W), BF16),
                   jax.ShapeDtypeStruct((bsz, 2, nslc, seq), BF16)),
        grid=(bsz, seq // qb),
        in_specs=[
            pl.BlockSpec((None, qb, GROUP_W), lambda b, i: (b, i, COL['b_q'] // GROUP_W)),
            pl.BlockSpec((None, ng, GROUP_W), lambda b, i: (b, 0, 0)),
            pl.BlockSpec((None, ng, GROUP_W), lambda b, i: (b, 0, 0)),
            pl.BlockSpec((nslc, ng), lambda b, i: (0, 0)),
        ],
        out_specs=(pl.BlockSpec((None, qb, GROUP_W), lambda b, i: (b, i, 0)),
                   pl.BlockSpec((None, 2, nslc, qb), lambda b, i: (b, 0, 0, i))),
        scratch_shapes=[pltpu.VMEM((2 * CMP_NSB, nslc, QBLK), F32)],
        compiler_params=_params(2),
        name="nsa_cmp_sel",
    )(u3, kc_rep, vc_rep, ovt)


SLC_KC = 512
SLC_QB = 256


def _slc_body(q_ref, sb_ref, ks_ref, vs_ref, o_ref, kaug, vaugt, m_s, acc_s, *, seq):
    i = pl.program_id(1)
    qb = SLC_QB
    q0 = i * qb
    kc = SLC_KC

    @pl.when(i == 0)
    def _():
        ch = 128
        lane = lax.broadcasted_iota(jnp.int32, (ch, 128), 1)
        lo = lane < HEAD_DIM
        top = lax.broadcasted_iota(jnp.int32, (128, ch), 0) < HEAD_DIM
        for c in range(seq // ch):
            rows = slice(c * ch, (c + 1) * ch)
            r = c * ch + lax.broadcasted_iota(jnp.int32, (ch, 128), 0)
            onehot = jnp.where((lane & (HEAD_DIM - 1)) == r // NSA_SLC_L, 1.0, 0.0)
            kf = ks_ref[rows, :].astype(F32)
            kaug[0, rows, :] = jnp.where(lo, kf, onehot).astype(BF16)
            kaug[1, rows, :] = jnp.where(lo, pltpu.roll(kf, HEAD_DIM, 1), onehot).astype(BF16)
            vt = vs_ref[rows, :].astype(F32).T
            cols = slice((c * ch) % kc, (c * ch) % kc + ch)
            vaugt[0, (c * ch) // kc, :, cols] = jnp.where(top, vt, 1.0).astype(BF16)
            vaugt[1, (c * ch) // kc, :, cols] = jnp.where(top, pltpu.roll(vt, HEAD_DIM, 0), 1.0).astype(BF16)

    qf = q_ref[...].astype(F32) * (HEAD_DIM ** -0.5)
    ws = []
    for hk in range(2):
        qt = qf[:, 128 * hk:128 * hk + 128].T.astype(BF16)
        bt = sb_ref[hk]
        ws.append(jnp.concatenate([jnp.concatenate([qt[0:HEAD_DIM], bt], axis=0),
                                   jnp.concatenate([qt[HEAD_DIM:128], bt], axis=0)], axis=1))

    m_s[...] = jnp.full(m_s.shape, NEG, F32)
    acc_s[...] = jnp.zeros(acc_s.shape, F32)

    rel = (lax.broadcasted_iota(jnp.int32, (kc, 2 * qb), 0)
           - (lax.broadcasted_iota(jnp.int32, (kc, 2 * qb), 1) & (qb - 1)))

    def scores(c, masked):
        out = []
        for hk in range(2):
            st = _dot(kaug[hk, c * kc:(c + 1) * kc, :], ws[hk])
            if masked:
                st = jnp.where(rel <= q0 - c * kc, st, NEG)
            out.append((st, jnp.max(st, axis=0, keepdims=True)))
        return out

    def consume(c, sm):
        m_old = [m_s[hk, 0:1, :] for hk in range(2)]
        m_new = [jnp.maximum(m_old[hk], sm[hk][1]) for hk in range(2)]
        pts = [jnp.exp(sm[hk][0] - m_new[hk]).astype(BF16) for hk in range(2)]
        pvs = [_dot(vaugt[hk, c], pts[hk]) for hk in range(2)]
        for hk in range(2):
            acc_s[hk] = acc_s[hk] * jnp.exp(m_old[hk] - m_new[hk]) + pvs[hk]
            m_s[hk] = jnp.broadcast_to(m_new[hk], (8, 2 * qb))

    n_last = (q0 + qb - 1) // kc

    def run(n_chunks):
        sm = scores(0, n_chunks == 1)
        for c in range(n_chunks):
            nxt = scores(c + 1, c + 2 == n_chunks) if c + 1 < n_chunks else None
            consume(c, sm)
            sm = nxt

    for n in range(seq // kc):
        pl.when(n_last == n)(functools.partial(run, n + 1))

    for hk in range(2):
        a = acc_s[hk]
        on = a[0:HEAD_DIM] * (1.0 / a[HEAD_DIM:128])
        stk = jnp.concatenate([on[:, 0:qb], on[:, qb:2 * qb]], axis=0)
        o_ref[:, 128 * hk:128 * hk + 128] = stk.T.astype(BF16)


def _slc_call(u3, sb, bsz, seq):
    nslc = seq // NSA_SLC_L
    return pl.pallas_call(
        functools.partial(_slc_body, seq=seq),
        out_shape=jax.ShapeDtypeStruct((bsz, seq, GROUP_W), BF16),
        grid=(bsz, seq // SLC_QB),
        in_specs=[
            pl.BlockSpec((None, SLC_QB, GROUP_W), lambda b, i: (b, i, COL['b_q'] // GROUP_W)),
            pl.BlockSpec((None, 2, nslc, SLC_QB), lambda b, i: (b, 0, 0, i)),
            pl.BlockSpec((None, seq, 128), lambda b, i: (b, 0, COL['b_ks'] // 128)),
            pl.BlockSpec((None, seq, 128), lambda b, i: (b, 0, COL['b_vs'] // 128)),
        ],
        out_specs=pl.BlockSpec((None, SLC_QB, GROUP_W), lambda b, i: (b, i, 0)),
        scratch_shapes=[pltpu.VMEM((2, seq, 128), BF16), pltpu.VMEM((2, seq // SLC_KC, 128, SLC_KC), BF16),
                        pltpu.VMEM((2, 8, 2 * SLC_QB), F32), pltpu.VMEM((2, 128, 2 * SLC_QB), F32)],
        compiler_params=_params(2),
        name="nsa_slc",
    )(u3, sb, u3, u3).reshape(bsz * seq, GROUP_W)


GLA_TC = 256
_GLA_NLEV = 6


def _gla_constants():
    c = GLA_CHUNK
    idx = np.arange(c)
    t = idx[None, :]
    mats = [(t <= idx[:, None])]
    masks = []
    for lev in range(_GLA_NLEV):
        m = (c // 2) >> lev
        blk = idx // m
        ref_q = blk * m
        ref_k = (blk + 1) * m
        odd = (blk % 2 == 1)
        dq = (t > ref_q[:, None]) & (t <= idx[:, None]) & odd[:, None]
        dk = (t > idx[:, None]) & (t <= ref_k[:, None]) & (~odd)[:, None]
        mats.append(dq | dk)
        same_parent = (idx[:, None] // (2 * m)) == (idx[None, :] // (2 * m))
        masks.append(same_parent & odd[:, None] & (~odd)[None, :])
    masks.append(np.eye(c, dtype=bool))
    dall = np.concatenate(mats, axis=0).astype(np.float32)
    mask4 = np.stack([np.tile(mk.astype(np.float32), (1, N_HEADS)) for mk in masks])
    hv = np.arange(GROUP_W) // HEAD_DIM
    hd = np.arange(N_HEADS * GLA_DK) // GLA_DK
    bd = (hv[:, None] == hd[None, :]).astype(np.float32)
    same_head = (hv[:, None] == hv[None, :]).astype(np.float32)
    return dall, mask4, bd, same_head


def _gla_body(cq_ref, ck_ref, cv_ref, misc_ref, wa_ref, ba_ref, dall_ref, mask_ref, bd_ref,
              sh_ref, gn_ref, o_ref, st):
    j = pl.program_id(1)

    @pl.when(j == 0)
    def _():
        st[...] = jnp.zeros_like(st)

    c = GLA_CHUNK
    nk = N_HEADS * GLA_DK
    lane_h = lax.broadcasted_iota(jnp.int32, (1, nk), 1) // GLA_DK
    lane_v = lax.broadcasted_iota(jnp.int32, (1, GROUP_W), 1) // HEAD_DIM
    dall = dall_ref[...]
    nch = GLA_TC // c
    rows = [slice(ch * c, (ch + 1) * c) for ch in range(nch)]
    z = _dot(misc_ref[...], wa_ref[...]) + ba_ref[...]
    la = (jnp.minimum(z, 0.0) - jnp.log1p(jnp.exp(-jnp.abs(z)))) * (1.0 / GLA_TAU)
    hi, lo = _split_hi_lo(la)
    hl = jnp.concatenate([hi, lo], axis=1)
    e_all = []
    for r in rows:
        e2 = _dot(dall, hl[r])
        e_all.append(e2[:, 0:nk] + e2[:, nk:2 * nk])
    qf = [cq_ref[r, :].astype(F32) * (GLA_DK ** -0.5) for r in rows]
    kf = [ck_ref[r, :].astype(F32) for r in rows]
    vs = [cv_ref[r, :] for r in rows]
    atts, q_in, kv_new, decay = [], [], [], []
    for ch in range(nch):
        bcum = e_all[ch][0:c]
        att = jnp.zeros((c, N_HEADS * c), F32)
        for lev in range(_GLA_NLEV + 1):
            if lev < _GLA_NLEV:
                ex = jnp.exp(e_all[ch][(1 + lev) * c:(2 + lev) * c])
                ql, kl = qf[ch] * ex, kf[ch] * ex
            else:
                ql, kl = qf[ch], kf[ch]
            klb = kl.astype(BF16)
            kbd = jnp.concatenate([jnp.where(lane_h == h, klb, jnp.zeros_like(klb))
                                   for h in range(N_HEADS)], axis=0)
            att = att + _dot_nt(ql.astype(BF16), kbd) * mask_ref[lev]
        atts.append(att.astype(BF16))
        q_in.append((qf[ch] * jnp.exp(bcum)).astype(BF16))
        b_last = bcum[c - 1:c, :]
        kt = (kf[ch] * jnp.exp(b_last - bcum)).astype(BF16)
        kv_new.append(bd_ref[...] * _dot_tn(vs[ch], kt))
        decay.append(jnp.exp(b_last))
    intra = []
    for ch in range(nch):
        vbd = jnp.concatenate([jnp.where(lane_v == h, vs[ch], jnp.zeros_like(vs[ch]))
                               for h in range(N_HEADS)], axis=0)
        intra.append(_dot(atts[ch], vbd))
    state = st[...]
    outs = []
    for ch in range(nch):
        outs.append(intra[ch] + _dot_nt(q_in[ch], state.astype(BF16)))
        state = state * decay[ch] + kv_new[ch]
    st[...] = state
    for ch in range(nch):
        o = outs[ch]
        ms = _dot_hl(o * o, sh_ref[...]) * (1.0 / HEAD_DIM)
        o_ref[rows[ch], :] = (o * lax.rsqrt(ms + EPS) * gn_ref[...]).astype(BF16)


def _gla_call(u3, wa_emb, ba, gnorm, bsz, seq):
    dall, mask4, bd, same_head = _gla_constants()
    tc = GLA_TC
    full = lambda shape: pl.BlockSpec(shape, lambda b, j: (0,) * len(shape))
    return pl.pallas_call(
        _gla_body,
        out_shape=jax.ShapeDtypeStruct((bsz, seq, GROUP_W), BF16),
        grid=(bsz, seq // tc),
        in_specs=[
            pl.BlockSpec((None, tc, 128), lambda b, j: (b, j, COL['c_q'] // 128)),
            pl.BlockSpec((None, tc, 128), lambda b, j: (b, j, COL['c_k'] // 128)),
            pl.BlockSpec((None, tc, GROUP_W), lambda b, j: (b, j, COL['c_v'] // GROUP_W)),
            pl.BlockSpec((None, tc, 128), lambda b, j: (b, j, COL['misc'] // 128)),
            full((128, 128)), full((1, 128)), full(dall.shape), full(mask4.shape),
            full(bd.shape), full(same_head.shape), full((1, GROUP_W)),
        ],
        out_specs=pl.BlockSpec((None, tc, GROUP_W), lambda b, j: (b, j, 0)),
        scratch_shapes=[pltpu.VMEM((GROUP_W, N_HEADS * GLA_DK), F32)],
        compiler_params=_params(2),
        name="gla",
    )(u3, u3, u3, u3, wa_emb, ba, jnp.asarray(dall, BF16), jnp.asarray(mask4, F32),
      jnp.asarray(bd, F32), jnp.asarray(same_head, BF16), gnorm).reshape(bsz * seq, GROUP_W)


OUT_TM = 512


def _gather_residues(src_ref, dst, r, rows, width):
    for j in range(r):
        for s in range(width // 128):
            dst[s, pl.ds(j, rows, stride=r), :] = (
                src_ref[:, j * width + 128 * s:j * width + 128 * (s + 1)].astype(F32))


def _out_body(x_ref, mod_ref, ln_ref, z_ref, o1_ref, o2_ref, o3_ref, l1_ref, l2_ref, l3_ref,
              ocmp_ref, oslc_ref, owin_ref, oc_ref, od_ref, misc_ref, gb_ref, e4_ref, eg_ref,
              w_ref, out_ref, so2, sl2, so3, sl3):
    tm = OUT_TM
    _gather_residues(o2_ref, so2, 4, tm // 4, GROUP_W)
    _gather_residues(l2_ref, sl2, 4, tm // 4, 128)
    _gather_residues(o3_ref, so3, 16, tm // 16, GROUP_W)
    _gather_residues(l3_ref, sl3, 16, tm // 16, 128)
    l1, l2, l3 = l1_ref[...], sl2[0], sl3[0]
    mx = jnp.maximum(jnp.maximum(l1, l2), l3)
    e1, e2, e3 = jnp.exp(l1 - mx), jnp.exp(l2 - mx), jnp.exp(l3 - mx)
    inv = 1.0 / (e1 + e2 + e3)
    e4 = e4_ref[...]
    pair_out = (o1_ref[...].astype(F32),
                jnp.concatenate([so2[0], so2[1]], axis=1),
                jnp.concatenate([so3[0], so3[1]], axis=1))
    oa = jnp.zeros((OUT_TM, GROUP_W), F32)
    for e, o in zip((e1, e2, e3), pair_out):
        oa = oa + _dot_hl2(e * inv, e4) * o
    sg = jax.nn.sigmoid(misc_ref[...].astype(F32) + gb_ref[...])
    sg_hi, sg_lo = _split_hi_lo(sg)
    sg2 = jnp.concatenate([sg_hi, sg_lo], axis=1)
    ob = jnp.zeros((OUT_TM, GROUP_W), F32)
    for br, o_ref in enumerate((ocmp_ref, oslc_ref, owin_ref)):
        ob = ob + _dot(sg2, eg_ref[br]) * o_ref[...].astype(F32)
    z = z_ref[...].astype(F32)
    sz = z * jax.nn.sigmoid(z)
    mixed = jnp.concatenate([oa, ob, oc_ref[...].astype(F32), od_ref[...].astype(F32)], axis=1) * sz
    y = _dot(mixed.astype(BF16), w_ref[...])
    ms = jnp.mean(y * y, axis=-1, keepdims=True)
    yn = y * lax.rsqrt(ms + EPS) * ln_ref[...]
    out_ref[...] = x_ref[...] + mod_ref[2:3, :] * yn


def _out_call(x2, mod3, ln_post, u2, o1, o2, o3, l1, l2, l3, ocmp, oslc, owin, oc, od,
              gate_b_row, e4, eg, w_out, seq):
    t, d = x2.shape
    tm = OUT_TM
    per_b = seq // tm
    row = lambda w: pl.BlockSpec((tm, w), lambda i: (i, 0))
    full = lambda shape: pl.BlockSpec(shape, lambda i: (0,) * len(shape))
    grouped = lambda r, w: pl.BlockSpec((None, tm // r, r * w), lambda i: (i // per_b, i % per_b, 0))
    return pl.pallas_call(
        _out_body,
        out_shape=jax.ShapeDtypeStruct((t, d), F32),
        grid=(t // tm,),
        in_specs=[
            row(d),
            pl.BlockSpec((None, 3, d), lambda i: (i // per_b, 0, 0)),
            full((1, d)),
            pl.BlockSpec((tm, 1024), lambda i: (i, COL['z'] // 1024)),
            row(GROUP_W), grouped(4, GROUP_W), grouped(16, GROUP_W),
            row(128), grouped(4, 128), grouped(16, 128),
            row(GROUP_W), row(GROUP_W), row(GROUP_W), row(GROUP_W), row(GROUP_W),
            pl.BlockSpec((tm, 128), lambda i: (i, COL['misc'] // 128)),
            full((1, 128)), full((256, GROUP_W)), full((3, 256, GROUP_W)), full((d, d)),
        ],
        out_specs=row(d),
        scratch_shapes=[pltpu.VMEM((2, tm, 128), F32), pltpu.VMEM((1, tm, 128), F32),
                        pltpu.VMEM((2, tm, 128), F32), pltpu.VMEM((1, tm, 128), F32)],
        compiler_params=_params(1),
        name="mix_out",
    )(x2, mod3, ln_post.reshape(1, d), u2, o1, o2, o3, l1, l2, l3, ocmp, oslc, owin, oc, od,
      u2, gate_b_row, e4, eg, w_out)


def _permute_columns(w_in):
    pieces, start = [], 0
    src = _SRC_COLS.tolist()
    while start < len(src):
        end = start + 1
        while end < len(src) and ((src[start] < 0 and src[end] < 0) or
                                  (src[start] >= 0 and src[end] == src[end - 1] + 1)):
            end += 1
        if src[start] < 0:
            pieces.append(jnp.zeros(w_in.shape[:-1] + (end - start,), w_in.dtype))
        else:
            pieces.append(w_in[..., src[start]:src[end - 1] + 1])
        start = end
    return jnp.concatenate(pieces, axis=-1)


def _expand_w1(w1):
    w = w1.reshape(2, NSA_CMP_D, HEAD_DIM, NSA_CMP_HID)
    zero = jnp.zeros((NSA_CMP_D, HEAD_DIM, NSA_CMP_HID), w1.dtype)
    top = jnp.concatenate([w[0], w[1], zero, zero], axis=-1)
    bot = jnp.concatenate([zero, zero, w[0], w[1]], axis=-1)
    return jnp.concatenate([top, bot], axis=1).astype(BF16)


def _compress_weights(w1k, w1v):
    wk, wv = _expand_w1(w1k), _expand_w1(w1v)
    zero = jnp.zeros_like(wk)
    rows_k = jnp.concatenate([wk, zero], axis=-1)
    rows_v = jnp.concatenate([zero, wv], axis=-1)
    return jnp.concatenate([rows_k, rows_v], axis=1).reshape(NSA_CMP_D * C_W, 1024)


def _expand_w2(w2):
    z = jnp.zeros_like(w2)
    h0 = jnp.concatenate([w2, w2, z, z], axis=1)
    h1 = jnp.concatenate([z, z, w2, w2], axis=1)
    return jnp.stack([h0, h1]).astype(BF16)


def _static_tables(seq):
    ng = seq // NSA_CMP_D
    nslc = seq // NSA_SLC_L
    cs = np.arange(ng) * NSA_CMP_D
    ss = np.arange(nslc) * NSA_SLC_L
    ovt = ((cs[None, :] < ss[:, None] + NSA_SLC_L) & (cs[None, :] + NSA_CMP_L > ss[:, None]))
    ovt[:, ng - 1] = False
    e4 = np.zeros((128, GROUP_W), np.float32)
    eg = np.zeros((3, 128, GROUP_W), np.float32)
    for h in range(N_HEADS):
        e4[32 * h, HEAD_DIM * h:HEAD_DIM * (h + 1)] = 1.0
        for br in range(3):
            eg[br, MISC_G_OFF + 3 * h + br, HEAD_DIM * h:HEAD_DIM * (h + 1)] = 1.0
    e4 = np.concatenate([e4, e4], axis=0)
    eg = np.concatenate([eg, eg], axis=1)
    return jnp.asarray(ovt, BF16), jnp.asarray(e4, BF16), jnp.asarray(eg, BF16)


def kernel(x, c, ln_pre, ln_post, w_ada, b_ada, w_in, w_out, nsa_pos_k, nsa_pos_v, nsa_w1_k, nsa_b1_k, nsa_w2_k, nsa_w1_v, nsa_b1_v, nsa_w2_v, nsa_gate_b, gla_w_a2, gla_b_a, gla_norm, sinks):
    bsz, seq, d = x.shape
    depth = w_in.shape[0]
    t = bsz * seq

    w_in_r = _permute_columns(w_in).astype(BF16)
    w_out_b = w_out.astype(BF16)
    ovt, e4, eg = _static_tables(seq)

    mod = _ada_call(c, w_ada, b_ada).reshape(depth, bsz, 3, d)

    x2 = x.reshape(t, d)
    for i in range(depth):
        u2, ua, ua4, ua16, uc16 = _proj_in_call(x2, mod[i], ln_pre[i], w_in_r[i], bsz, seq)
        u3 = u2.reshape(bsz, seq, U_W)

        grouped_a = {1: ua.reshape(bsz, seq, A_W), 4: ua4, 16: ua16}
        dil = [_dil_pair_call(grouped_a[r], bsz, seq, w, r) for w, r in DIL_PAIRS]

        kc_rep, vc_rep = _compress_call(
            uc16, bsz, seq, _compress_weights(nsa_w1_k[i], nsa_w1_v[i]),
            jnp.broadcast_to(nsa_pos_k[i].reshape(1, -1), (8, NSA_CMP_L * HEAD_DIM)).astype(BF16),
            jnp.broadcast_to(nsa_pos_v[i].reshape(1, -1), (8, NSA_CMP_L * HEAD_DIM)).astype(BF16),
            nsa_w1_k[i].astype(BF16), nsa_w1_v[i].astype(BF16),
            nsa_b1_k[i].reshape(1, -1), nsa_b1_v[i].reshape(1, -1),
            _expand_w2(nsa_w2_k[i]), _expand_w2(nsa_w2_v[i]))
        ocmp, sb = _cmp_sel_call(u3, kc_rep, vc_rep, ovt, bsz, seq)
        oslc = _slc_call(u3, sb, bsz, seq)
        owin = _win_call(u3, bsz, seq)

        wa_emb = jnp.zeros((128, 128), F32).at[MISC_A_OFF:MISC_A_OFF + GLA_RANK, :].set(gla_w_a2[i]).astype(BF16)
        oc = _gla_call(u3, wa_emb, gla_b_a[i].reshape(1, -1), gla_norm[i].reshape(1, -1), bsz, seq)

        od = _swa_call(u3, sinks[i], bsz, seq)

        gate_b_row = jnp.zeros((1, 128), F32).at[0, MISC_G_OFF:MISC_G_OFF + 3 * N_HEADS].set(nsa_gate_b[i])
        x2 = _out_call(x2, mod[i], ln_post[i], u2, dil[0][0].reshape(t, GROUP_W), dil[1][0], dil[2][0],
                       dil[0][1].reshape(t, 128), dil[1][1], dil[2][1], ocmp.reshape(t, GROUP_W), oslc, owin, oc, od,
                       gate_b_row, e4, eg, w_out_b[i], seq)
    return x2.reshape(bsz, seq, d)
```

```python
import functools

import numpy as np
import jax
import jax.numpy as jnp
from jax import lax
from jax.experimental import pallas as pl
from jax.experimental.pallas import tpu as pltpu

F32 = jnp.float32
BF16 = jnp.bfloat16

D_MODEL = 1024
HEAD_DIM = 64
N_HEADS = 4
GROUP_W = N_HEADS * HEAD_DIM
EPS = 1e-6
NEG = -1e30
QBLK = 128

DIL_PAIRS = ((128, 1), (512, 4), (2048, 16))
NSA_CMP_L = 32
NSA_CMP_D = 16
NSA_CMP_HID = 128
NSA_SLC_L = 64
NSA_TOP_N = 16
NSA_WIN = 512
GLA_DK = 32
GLA_RANK = 16
GLA_TAU = 16.0
GLA_CHUNK = 64
SWA_WIN = 128

VMEM_LIMIT = 56 * 1024 * 1024

_ORIG_SPLITS = (
    ('a_q', 256), ('a_k', 256), ('a_v', 256), ('a_z', 256),
    ('b_q', 256), ('b_kc', 128), ('b_vc', 128), ('b_ks', 128), ('b_vs', 128),
    ('b_kw', 128), ('b_vw', 128), ('b_g', 12), ('b_z', 256),
    ('c_q', 128), ('c_k', 128), ('c_v', 256), ('c_a', 16), ('c_z', 256),
    ('d_q', 256), ('d_k', 64), ('d_v', 64), ('d_z', 256),
)
_NEW_ORDER = (
    ('a_q', ('a_q',), 256), ('a_k', ('a_k',), 256), ('a_v', ('a_v',), 256),
    ('b_kc', ('b_kc',), 128), ('b_vc', ('b_vc',), 128),
    ('z', ('a_z', 'b_z', 'c_z', 'd_z'), 1024),
    ('b_q', ('b_q',), 256), ('d_q', ('d_q',), 256), ('c_v', ('c_v',), 256),
    ('b_ks', ('b_ks',), 128), ('b_vs', ('b_vs',), 128), ('b_kw', ('b_kw',), 128),
    ('b_vw', ('b_vw',), 128), ('c_q', ('c_q',), 128), ('c_k', ('c_k',), 128),
    ('d_kv', ('d_k', 'd_v'), 128), ('misc', ('b_g', 'c_a'), 128),
)
A_W = 768
C_W = 256
MISC_G_OFF = 0
MISC_A_OFF = 12


def _build_layout():
    off, o = {}, 0
    for name, w in _ORIG_SPLITS:
        off[name] = (o, w)
        o += w
    src, col, n = [], {}, 0
    for name, parts, width in _NEW_ORDER:
        col[name] = n
        used = 0
        for p in parts:
            s, w = off[p]
            src.extend(range(s, s + w))
            used += w
        src.extend([-1] * (width - used))
        n += width
    return np.asarray(src, np.int32), col, n


_SRC_COLS, COL, U_W = _build_layout()
assert COL['a_q'] == 0 and COL['b_kc'] == A_W and COL['z'] == A_W + C_W


def _dot(a, b):
    return jnp.dot(a, b, preferred_element_type=F32)


def _dot_nt(a, b):
    return lax.dot_general(a, b, (((1,), (1,)), ((), ())), preferred_element_type=F32)


def _dot_tn(a, b):
    return lax.dot_general(a, b, (((0,), (0,)), ((), ())), preferred_element_type=F32)


def _split_hi_lo(x):
    hi = x.astype(BF16)
    lo = (x - hi.astype(F32)).astype(BF16)
    return hi, lo


def _dot_hl(x, w):
    hi, lo = _split_hi_lo(x)
    return _dot(hi, w) + _dot(lo, w)


def _dot_hl2(x, w2):
    hi, lo = _split_hi_lo(x)
    return _dot(jnp.concatenate([hi, lo], axis=1), w2)


def _params(n_axes):
    return pltpu.CompilerParams(dimension_semantics=("arbitrary",) * n_axes,
                                vmem_limit_bytes=VMEM_LIMIT)


def _ada_body(c_ref, w_ref, b_ref, o_ref):
    c = c_ref[...]
    sc = (c * jax.nn.sigmoid(c)).astype(BF16)
    o_ref[...] = _dot(sc, w_ref[...].astype(BF16)) + b_ref[...]


def _ada_call(c, w_ada, b_ada):
    depth, d, d3 = w_ada.shape
    b = c.shape[0]
    nb = d3 // d
    return pl.pallas_call(
        _ada_body,
        out_shape=jax.ShapeDtypeStruct((depth, b, d3), F32),
        grid=(depth, nb),
        in_specs=[
            pl.BlockSpec((b, d), lambda i, n: (0, 0)),
            pl.BlockSpec((None, d, d), lambda i, n: (i, 0, n)),
            pl.BlockSpec((None, 1, d), lambda i, n: (i, 0, n)),
        ],
        out_specs=pl.BlockSpec((None, b, d), lambda i, n: (i, 0, n)),
        compiler_params=_params(2),
        name="ada_mod",
    )(c, w_ada, b_ada.reshape(depth, 1, d3))


PROJ_TM = 512
_PROJ_CHUNKS = ((0, 768), (768, 1024), (1024, 1792), (1792, 2560), (2560, 3328), (3328, 3840))


def _scatter_4_then_16(src, dst4_ref, dst16_ref, mid, rows4, width):
    ns = width // 128
    for j4 in range(4):
        for s in range(ns):
            v = src[s, pl.ds(j4, rows4, stride=4), :]
            if dst4_ref is not None:
                dst4_ref[:, j4 * width + 128 * s:j4 * width + 128 * (s + 1)] = v.astype(dst4_ref.dtype)
            mid[j4 * ns + s] = v
    for j4 in range(4):
        for e in range(4):
            for s in range(ns):
                c0 = (4 * e + j4) * width + 128 * s
                dst16_ref[:, c0:c0 + 128] = mid[j4 * ns + s, pl.ds(e, rows4 // 4, stride=4), :].astype(dst16_ref.dtype)


def _proj_in_body(x_ref, mod_ref, ln_ref, w_ref, o_ref, oa_ref, oa4_ref, oa16_ref, oc16_ref, sa, sc, mida, midc):
    x = x_ref[...]
    ms = jnp.mean(x * x, axis=-1, keepdims=True)
    y = x * lax.rsqrt(ms + EPS) * ln_ref[...]
    shift = mod_ref[0:1, :]
    scale = mod_ref[1:2, :]
    h = (y * (1.0 + scale) + shift).astype(BF16)
    tm = PROJ_TM
    for n0, n1 in _PROJ_CHUNKS:
        res = _dot(h, w_ref[:, n0:n1])
        r = res.astype(BF16)
        o_ref[:, n0:n1] = r
        if (n0, n1) == (0, A_W):
            oa_ref[...] = r
            for s in range(A_W // 128):
                sa[s] = res[:, 128 * s:128 * (s + 1)]
            _scatter_4_then_16(sa, oa4_ref, oa16_ref, mida, tm // 4, A_W)
        elif (n0, n1) == (A_W, A_W + C_W):
            for s in range(C_W // 128):
                sc[s] = res[:, 128 * s:128 * (s + 1)]
            _scatter_4_then_16(sc, None, oc16_ref, midc, tm // 4, C_W)


def _proj_in_call(x2, mod3, ln_pre, w_r, bsz, seq):
    t, d = x2.shape
    tm = PROJ_TM
    per_b = seq // tm
    grouped = lambda r, w: pl.BlockSpec((None, tm // r, r * w), lambda i: (i // per_b, i % per_b, 0))
    return pl.pallas_call(
        _proj_in_body,
        out_shape=(jax.ShapeDtypeStruct((t, U_W), BF16),
                   jax.ShapeDtypeStruct((t, A_W), BF16),
                   jax.ShapeDtypeStruct((bsz, seq // 4, 4 * A_W), BF16),
                   jax.ShapeDtypeStruct((bsz, seq // 16, 16 * A_W), BF16),
                   jax.ShapeDtypeStruct((bsz, seq // NSA_CMP_D, NSA_CMP_D * C_W), BF16)),
        grid=(t // tm,),
        in_specs=[
            pl.BlockSpec((tm, d), lambda i: (i, 0)),
            pl.BlockSpec((None, 3, d), lambda i: (i // per_b, 0, 0)),
            pl.BlockSpec((1, d), lambda i: (0, 0)),
            pl.BlockSpec((d, U_W), lambda i: (0, 0)),
        ],
        out_specs=(pl.BlockSpec((tm, U_W), lambda i: (i, 0)),
                   pl.BlockSpec((tm, A_W), lambda i: (i, 0)),
                   grouped(4, A_W), grouped(16, A_W), grouped(NSA_CMP_D, C_W)),
        scratch_shapes=[pltpu.VMEM((A_W // 128, tm, 128), F32), pltpu.VMEM((C_W // 128, tm, 128), F32),
                        pltpu.VMEM((4 * A_W // 128, tm // 4, 128), F32),
                        pltpu.VMEM((4 * C_W // 128, tm // 4, 128), F32)],
        compiler_params=_params(1),
        name="proj_in",
    )(x2, mod3, ln_pre.reshape(1, d), w_r)


def _stack_heads(q):
    lane_h = lax.broadcasted_iota(jnp.int32, (1, GROUP_W), 1) // HEAD_DIM
    zero = jnp.zeros_like(q)
    return jnp.concatenate([jnp.where(lane_h == h, q, zero) for h in range(N_HEADS)], axis=0)


def _pick_heads(r):
    lane_h = lax.broadcasted_iota(jnp.int32, (1, GROUP_W), 1) // HEAD_DIM
    out = jnp.where(lane_h == 0, r[0], 0.0)
    for h in range(1, N_HEADS):
        out = jnp.where(lane_h == h, r[h], out)
    return out


def _band_bias_table(blk, span, nb, back):
    d = np.arange(nb + 1)[:, None, None] * blk + np.arange(blk)[None, :, None] - np.arange(span)[None, None, :]
    return jnp.asarray(np.where((d >= 0) & (d <= back), 0.0, NEG), F32)


def _banded_core(i, q_ref, k_src, v_src, bias_ref, o_ref, lse_ref, sink_ref, *, blk, nsub, span, nb, back):
    if sink_ref is not None:
        hidx = lax.broadcasted_iota(jnp.int32, (N_HEADS, 1, 1), 0)
        sink = jnp.zeros((N_HEADS, 1, 1), F32)
        for h in range(N_HEADS):
            sink = jnp.where(hidx == h, sink_ref[h], sink)
    lane_l = lax.broadcasted_iota(jnp.int32, (1, 128), 1) // 32

    def window(sb):
        ib = i * nsub + sb
        return ib, pl.multiple_of(jnp.maximum(ib - nb, 0) * blk, blk)

    def qk(sb):
        ib, start = window(sb)
        q = q_ref[sb * blk:(sb + 1) * blk, :] * jnp.asarray(HEAD_DIM ** -0.5, BF16)
        s = _dot_nt(_stack_heads(q), k_src[pl.ds(start, span), :]).reshape(N_HEADS, blk, span)
        return s + bias_ref[jnp.minimum(ib, nb)][None]

    s_next = qk(0)
    for sb in range(nsub):
        s = s_next
        s_next = qk(sb + 1) if sb + 1 < nsub else None
        vb = v_src[pl.ds(window(sb)[1], span), :]
        m = jnp.max(s, axis=-1, keepdims=True)
        if sink_ref is not None:
            m = jnp.maximum(m, sink)
        p = jnp.exp(s - m)
        l = jnp.sum(p, axis=-1, keepdims=True)
        den = l if sink_ref is None else l + jnp.exp(sink - m)
        r = _dot(p.reshape(N_HEADS * blk, span).astype(BF16), vb).reshape(N_HEADS, blk, GROUP_W)
        o_ref[sb * blk:(sb + 1) * blk, :] = _pick_heads(r * (1.0 / den)).astype(o_ref.dtype)
        if lse_ref is not None:
            lse = m + jnp.log(l)
            acc = jnp.zeros((blk, 128), F32)
            for h in range(N_HEADS):
                acc = jnp.where(lane_l == h, lse[h], acc)
            lse_ref[sb * blk:(sb + 1) * blk, :] = acc


def _dil_body(q_ref, k_ref, v_ref, bias_ref, o_ref, lse_ref, **kw):
    _banded_core(pl.program_id(2), q_ref, k_ref, v_ref, bias_ref, o_ref, lse_ref, None, **kw)


def _dil_multi_body(qkv_ref, bias_ref, o_ref, lse_ref, *, nres, **kw):
    for jj in range(nres):
        c0 = jj * A_W
        _banded_core(0, qkv_ref.at[:, c0:c0 + GROUP_W], qkv_ref.at[:, c0 + GROUP_W:c0 + 2 * GROUP_W],
                     qkv_ref.at[:, c0 + 2 * GROUP_W:c0 + 3 * GROUP_W], bias_ref,
                     o_ref.at[:, jj * GROUP_W:(jj + 1) * GROUP_W], lse_ref.at[:, jj * 128:(jj + 1) * 128],
                     None, **kw)


DIL_RES_PER_STEP = 4
BAND_MAX_SUB = 8


def _banded_geometry(ln, back):
    blk = min(QBLK, ln)
    nb = -(-back // blk)
    span = min((nb + 1) * blk, ln)
    nsub = next(n for n in (BAND_MAX_SUB, 4, 2, 1) if ln % (n * blk) == 0)
    return blk, nb, span, nsub


def _dil_pair_call(uv, bsz, seq, w, r):
    ln = seq // r
    back = w // r
    blk, nb, span, nsub = _banded_geometry(ln, back)
    rows = blk * nsub
    cpb = A_W // GROUP_W
    nres = DIL_RES_PER_STEP
    if rows == ln and r % nres == 0:
        return pl.pallas_call(
            functools.partial(_dil_multi_body, nres=nres, blk=blk, nsub=nsub, span=span, nb=nb, back=back),
            out_shape=(jax.ShapeDtypeStruct((bsz, ln, r * GROUP_W), BF16),
                       jax.ShapeDtypeStruct((bsz, ln, r * 128), F32)),
            grid=(bsz, r // nres),
            in_specs=[
                pl.BlockSpec((None, ln, nres * A_W), lambda b, j: (b, 0, j)),
                pl.BlockSpec((nb + 1, blk, span), lambda b, j: (0, 0, 0)),
            ],
            out_specs=(pl.BlockSpec((None, ln, nres * GROUP_W), lambda b, j: (b, 0, j)),
                       pl.BlockSpec((None, ln, nres * 128), lambda b, j: (b, 0, j))),
            compiler_params=_params(2),
            name=f"dil_r{r}",
        )(uv, _band_bias_table(blk, span, nb, back))
    o, lse = pl.pallas_call(
        functools.partial(_dil_body, blk=blk, nsub=nsub, span=span, nb=nb, back=back),
        out_shape=(jax.ShapeDtypeStruct((bsz, ln, r * GROUP_W), BF16),
                   jax.ShapeDtypeStruct((bsz, ln, r * 128), F32)),
        grid=(bsz, r, ln // rows),
        in_specs=[
            pl.BlockSpec((None, rows, GROUP_W), lambda b, j, i: (b, i, j * cpb)),
            pl.BlockSpec((None, ln, GROUP_W), lambda b, j, i: (b, 0, j * cpb + 1)),
            pl.BlockSpec((None, ln, GROUP_W), lambda b, j, i: (b, 0, j * cpb + 2)),
            pl.BlockSpec((nb + 1, blk, span), lambda b, j, i: (0, 0, 0)),
        ],
        out_specs=(pl.BlockSpec((None, rows, GROUP_W), lambda b, j, i: (b, i, j)),
                   pl.BlockSpec((None, rows, 128), lambda b, j, i: (b, i, j))),
        compiler_params=_params(3),
        name=f"dil_r{r}",
    )(uv, uv, uv, _band_bias_table(blk, span, nb, back))
    return o, lse


def _rep_prep(src_refs, dst_refs, seq, mode):
    ch = 512
    lane = lax.broadcasted_iota(jnp.int32, (ch, 128), 1)
    lo = lane < HEAD_DIM
    for c in range(seq // ch):
        rows = slice(c * ch, (c + 1) * ch)
        if mode == 'swa':
            x = src_refs[0][rows, :].astype(F32)
            sw = pltpu.roll(x, HEAD_DIM, 1)
            kk = jnp.where(lo, x, sw).astype(BF16)
            vv = jnp.where(lo, sw, x).astype(BF16)
            dst_refs[0][rows, 0:128] = kk
            dst_refs[0][rows, 128:256] = kk
            dst_refs[1][rows, 0:128] = vv
            dst_refs[1][rows, 128:256] = vv
        else:
            for s_ref, d_ref in zip(src_refs, dst_refs):
                x = s_ref[rows, :].astype(F32)
                sw = pltpu.roll(x, HEAD_DIM, 1)
                d_ref[rows, 0:128] = jnp.where(lo, x, sw).astype(BF16)
                d_ref[rows, 128:256] = jnp.where(lo, sw, x).astype(BF16)


def _swa_body(sink_ref, q_ref, kv_ref, bias_ref, o_ref, kr, vr, *, seq, **kw):
    i = pl.program_id(1)

    @pl.when(i == 0)
    def _():
        _rep_prep((kv_ref,), (kr, vr), seq, 'swa')

    _banded_core(i, q_ref, kr, vr, bias_ref, o_ref, None, sink_ref, **kw)


def _win_body(q_ref, kw_ref, vw_ref, bias_ref, o_ref, kr, vr, *, seq, **kw):
    i = pl.program_id(1)

    @pl.when(i == 0)
    def _():
        _rep_prep((kw_ref, vw_ref), (kr, vr), seq, 'win')

    _banded_core(i, q_ref, kr, vr, bias_ref, o_ref, None, None, **kw)


def _shared_kv_call(name, body, operands, specs, q_col, bsz, seq, back):
    blk, nb, span, nsub = _banded_geometry(seq, back)
    rows = blk * nsub
    return pl.pallas_call(
        functools.partial(body, seq=seq, blk=blk, nsub=nsub, span=span, nb=nb, back=back),
        out_shape=jax.ShapeDtypeStruct((bsz, seq, GROUP_W), BF16),
        grid=(bsz, seq // rows),
        in_specs=(specs[0] + [pl.BlockSpec((None, rows, GROUP_W), lambda b, i: (b, i, q_col // GROUP_W))]
                  + specs[1] + [pl.BlockSpec((nb + 1, blk, span), lambda b, i: (0, 0, 0))]),
        out_specs=pl.BlockSpec((None, rows, GROUP_W), lambda b, i: (b, i, 0)),
        scratch_shapes=[pltpu.VMEM((seq, GROUP_W), BF16), pltpu.VMEM((seq, GROUP_W), BF16)],
        compiler_params=_params(2),
        name=name,
    )(*operands, _band_bias_table(blk, span, nb, back)).reshape(bsz * seq, GROUP_W)


def _swa_call(u3, sinks, bsz, seq):
    kv = lambda col: pl.BlockSpec((None, seq, 128), lambda b, i: (b, 0, col // 128))
    return _shared_kv_call("swa", _swa_body, (sinks, u3, u3),
                           ([pl.BlockSpec(memory_space=pltpu.SMEM)], [kv(COL['d_kv'])]),
                           COL['d_q'], bsz, seq, SWA_WIN - 1)


def _win_call(u3, bsz, seq):
    kv = lambda col: pl.BlockSpec((None, seq, 128), lambda b, i: (b, 0, col // 128))
    return _shared_kv_call("nsa_win", _win_body, (u3, u3, u3),
                           ([], [kv(COL['b_kw']), kv(COL['b_vw'])]),
                           COL['b_q'], bsz, seq, NSA_WIN - 1)


def _compress_body(x_ref, w_ref, pk_ref, pv_ref, w1k_ref, w1v_ref,
                   b1k_ref, b1v_ref, w2k_ref, w2v_ref, ok_ref, ov_ref, *, ng):
    acc_kv = _dot(x_ref[...], w_ref[...])
    for off, p_ref, w1_ref, b1_ref, w2_ref, o_ref in (
            (0, pk_ref, w1k_ref, b1k_ref, w2k_ref, ok_ref),
            (512, pv_ref, w1v_ref, b1v_ref, w2v_ref, ov_ref)):
        const = _dot(p_ref[...], w1_ref[...])[0:1, :] + b1_ref[...]
        out = jnp.zeros((ng, GROUP_W), F32)
        for h in range(2):
            first = acc_kv[:, off + 256 * h:off + 256 * h + 128]
            second = acc_kv[:, off + 256 * h + 128:off + 256 * h + 256]
            pre = first + pltpu.roll(second, ng - 1, 0) + const
            act = jax.nn.gelu(pre, approximate=True)
            out = out + _dot(act.astype(BF16), w2_ref[h])
        o_ref[...] = out.astype(BF16)


def _compress_call(uc16, bsz, seq, w_all, posk, posv, w1k, w1v, b1k, b1v, w2k_exp, w2v_exp):
    ng = seq // NSA_CMP_D
    full = lambda shape: pl.BlockSpec(shape, lambda b: (0,) * len(shape))
    return pl.pallas_call(
        functools.partial(_compress_body, ng=ng),
        out_shape=(jax.ShapeDtypeStruct((bsz, ng, GROUP_W), BF16),
                   jax.ShapeDtypeStruct((bsz, ng, GROUP_W), BF16)),
        grid=(bsz,),
        in_specs=[
            pl.BlockSpec((None, ng, NSA_CMP_D * C_W), lambda b: (b, 0, 0)),
            full((NSA_CMP_D * C_W, 1024)),
            full((8, NSA_CMP_L * HEAD_DIM)), full((8, NSA_CMP_L * HEAD_DIM)),
            full((NSA_CMP_L * HEAD_DIM, NSA_CMP_HID)), full((NSA_CMP_L * HEAD_DIM, NSA_CMP_HID)),
            full((1, NSA_CMP_HID)), full((1, NSA_CMP_HID)),
            full((2, NSA_CMP_HID, GROUP_W)), full((2, NSA_CMP_HID, GROUP_W)),
        ],
        out_specs=(pl.BlockSpec((None, ng, GROUP_W), lambda b: (b, 0, 0)),
                   pl.BlockSpec((None, ng, GROUP_W), lambda b: (b, 0, 0))),
        compiler_params=_params(1),
        name="nsa_compress",
    )(uc16, w_all, posk, posv, w1k, w1v, b1k, b1v, w2k_exp, w2v_exp)


CMP_NSB = 4


def _cmp_sel_body(q_ref, kc_ref, vc_ref, ovt_ref, o_ref, sb_ref, scr, *, ng, nslc):
    i = pl.program_id(1)
    q0 = i * (CMP_NSB * QBLK)
    ci = lax.broadcasted_iota(jnp.int32, (QBLK, ng), 1)
    trow = lax.broadcasted_iota(jnp.int32, (QBLK, ng), 0)
    jrow = lax.broadcasted_iota(jnp.int32, (nslc, QBLK), 0)
    qcol = lax.broadcasted_iota(jnp.int32, (nslc, QBLK), 1)
    ovt = ovt_ref[...]

    def qk(sb):
        q = q_ref[sb * QBLK:(sb + 1) * QBLK, :] * jnp.asarray(HEAD_DIM ** -0.5, BF16)
        return _dot_nt(_stack_heads(q), kc_ref[...]).reshape(N_HEADS, QBLK, ng)

    forced, causal = [], []
    s_next = qk(0)
    for sb in range(CMP_NSB):
        s_raw = s_next
        s_next = qk(sb + 1) if sb + 1 < CMP_NSB else None
        qs = q0 + sb * QBLK
        valid = ((ci * NSA_CMP_D + (NSA_CMP_L - 1)) <= qs + trow)[None]
        s = jnp.where(valid, s_raw, NEG)
        m = jnp.max(s, axis=-1, keepdims=True)
        e = jnp.where(valid, jnp.exp(s - m), 0.0)
        l = jnp.sum(e, axis=-1, keepdims=True)
        p = e * (1.0 / jnp.maximum(l, 1e-30))
        r = _dot(p.reshape(N_HEADS * QBLK, ng).astype(BF16), vc_ref[...]).reshape(N_HEADS, QBLK, GROUP_W)
        o_ref[sb * QBLK:(sb + 1) * QBLK, :] = _pick_heads(r).astype(BF16)

        cur = (qs + qcol) // NSA_SLC_L
        forced.append(jnp.where(jrow == 0, 1.0, jnp.where(jrow == cur, 1.0, jnp.where(jrow == cur - 1, 1.0, 0.0))))
        causal.append(jnp.where(jrow <= cur, 1.0, 0.0))
        for hk in range(2):
            psum = p[2 * hk] + p[2 * hk + 1]
            hi, lo = _split_hi_lo(psum)
            imp_t = _dot_nt(ovt, hi) + _dot_nt(ovt, lo)
            scr[2 * sb + hk] = jnp.where(causal[sb] * (1.0 - forced[sb]) > 0.5, imp_t, -1.0)

    n_free = NSA_TOP_N - 3
    nv = nslc // 8
    sub = lax.broadcasted_iota(jnp.int32, (8, QBLK), 0)
    tier = (q0 + CMP_NSB * QBLK - 1) // (8 * NSA_SLC_L)
    for tt in range(nv):
        @pl.when(tier == tt)
        def _(tt=tt):
            nvt = tt + 1
            for sb in range(CMP_NSB):
                for hk in range(2):
                    k = 2 * sb + hk
                    vals = [scr[k, 8 * v:8 * v + 8, :] for v in range(nvt)]
                    cnts = [jnp.zeros((8, QBLK), F32) for _ in range(nvt)]
                    for jp in range(8 * nvt):
                        rv = jnp.broadcast_to(scr[k, jp:jp + 1, :], (8, QBLK))
                        for v in range(nvt):
                            if 8 * v > jp:
                                beats = jnp.where(rv >= vals[v], 1.0, 0.0)
                            elif 8 * v + 7 <= jp:
                                beats = jnp.where(rv > vals[v], 1.0, 0.0)
                            else:
                                tie = jnp.where(sub + 8 * v > jp, jnp.where(rv == vals[v], 1.0, 0.0), 0.0)
                                beats = jnp.where(rv > vals[v], 1.0, 0.0) + tie
                            cnts[v] = cnts[v] + beats
                    pieces = [jnp.where(c < n_free - 0.5, 1.0, 0.0) for c in cnts]
                    if nvt < nv:
                        pieces.append(jnp.zeros((8 * (nv - nvt), QBLK), F32))
                    picked = jnp.concatenate(pieces, axis=0)
                    sel_t = causal[sb] * jnp.maximum(forced[sb], picked)
                    sb_ref[hk, :, sb * QBLK:(sb + 1) * QBLK] = jnp.where(sel_t > 0.5, 0.0, NEG).astype(BF16)


def _cmp_sel_call(u3, kc_rep, vc_rep, ovt, bsz, seq):
    ng = seq // NSA_CMP_D
    nslc = seq // NSA_SLC_L
    assert nslc == 64, "selection-bias layout assumes 64 selection blocks"
    qb = CMP_NSB * QBLK
    return pl.pallas_call(
        functools.partial(_cmp_sel_body, ng=ng, nslc=nslc),
        out_shape=(jax.ShapeDtypeStruct((bsz, seq, GROUP_W), BF16),
                   jax.ShapeDtypeStruct((bsz, 2, nslc, seq), BF16)),
        grid=(bsz, seq // qb),
        in_specs=[
            pl.BlockSpec((None, qb, GROUP_W), lambda b, i: (b, i, COL['b_q'] // GROUP_W)),
            pl.BlockSpec((None, ng, GROUP_W), lambda b, i: (b, 0, 0)),
            pl.BlockSpec((None, ng, GROUP_W), lambda b, i: (b, 0, 0)),
            pl.BlockSpec((nslc, ng), lambda b, i: (0, 0)),
        ],
        out_specs=(pl.BlockSpec((None, qb, GROUP_W), lambda b, i: (b, i, 0)),
                   pl.BlockSpec((None, 2, nslc, qb), lambda b, i: (b, 0, 0, i))),
        scratch_shapes=[pltpu.VMEM((2 * CMP_NSB, nslc, QBLK), F32)],
        compiler_params=_params(2),
        name="nsa_cmp_sel",
    )(u3, kc_rep, vc_rep, ovt)


SLC_KC = 512
SLC_QB = 256


def _slc_body(q_ref, sb_ref, ks_ref, vs_ref, o_ref, kaug, vaugt, m_s, acc_s, *, seq):
    i = pl.program_id(1)
    qb = SLC_QB
    q0 = i * qb
    kc = SLC_KC

    @pl.when(i == 0)
    def _():
        ch = 128
        lane = lax.broadcasted_iota(jnp.int32, (ch, 128), 1)
        lo = lane < HEAD_DIM
        top = lax.broadcasted_iota(jnp.int32, (128, ch), 0) < HEAD_DIM
        for c in range(seq // ch):
            rows = slice(c * ch, (c + 1) * ch)
            r = c * ch + lax.broadcasted_iota(jnp.int32, (ch, 128), 0)
            onehot = jnp.where((lane & (HEAD_DIM - 1)) == r // NSA_SLC_L, 1.0, 0.0)
            kf = ks_ref[rows, :].astype(F32)
            kaug[0, rows, :] = jnp.where(lo, kf, onehot).astype(BF16)
            kaug[1, rows, :] = jnp.where(lo, pltpu.roll(kf, HEAD_DIM, 1), onehot).astype(BF16)
            vt = vs_ref[rows, :].astype(F32).T
            cols = slice((c * ch) % kc, (c * ch) % kc + ch)
            vaugt[0, (c * ch) // kc, :, cols] = jnp.where(top, vt, 1.0).astype(BF16)
            vaugt[1, (c * ch) // kc, :, cols] = jnp.where(top, pltpu.roll(vt, HEAD_DIM, 0), 1.0).astype(BF16)

    qf = q_ref[...].astype(F32) * (HEAD_DIM ** -0.5)
    ws = []
    for hk in range(2):
        qt = qf[:, 128 * hk:128 * hk + 128].T.astype(BF16)
        bt = sb_ref[hk]
        ws.append(jnp.concatenate([jnp.concatenate([qt[0:HEAD_DIM], bt], axis=0),
                                   jnp.concatenate([qt[HEAD_DIM:128], bt], axis=0)], axis=1))

    m_s[...] = jnp.full(m_s.shape, NEG, F32)
    acc_s[...] = jnp.zeros(acc_s.shape, F32)

    rel = (lax.broadcasted_iota(jnp.int32, (kc, 2 * qb), 0)
           - (lax.broadcasted_iota(jnp.int32, (kc, 2 * qb), 1) & (qb - 1)))

    def scores(c, masked):
        out = []
        for hk in range(2):
            st = _dot(kaug[hk, c * kc:(c + 1) * kc, :], ws[hk])
            if masked:
                st = jnp.where(rel <= q0 - c * kc, st, NEG)
            out.append((st, jnp.max(st, axis=0, keepdims=True)))
        return out

    def consume(c, sm):
        m_old = [m_s[hk, 0:1, :] for hk in range(2)]
        m_new = [jnp.maximum(m_old[hk], sm[hk][1]) for hk in range(2)]
        pts = [jnp.exp(sm[hk][0] - m_new[hk]).astype(BF16) for hk in range(2)]
        pvs = [_dot(vaugt[hk, c], pts[hk]) for hk in range(2)]
        for hk in range(2):
            acc_s[hk] = acc_s[hk] * jnp.exp(m_old[hk] - m_new[hk]) + pvs[hk]
            m_s[hk] = jnp.broadcast_to(m_new[hk], (8, 2 * qb))

    n_last = (q0 + qb - 1) // kc

    def run(n_chunks):
        sm = scores(0, n_chunks == 1)
        for c in range(n_chunks):
            nxt = scores(c + 1, c + 2 == n_chunks) if c + 1 < n_chunks else None
            consume(c, sm)
            sm = nxt

    for n in range(seq // kc):
        pl.when(n_last == n)(functools.partial(run, n + 1))

    for hk in range(2):
        a = acc_s[hk]
        on = a[0:HEAD_DIM] * (1.0 / a[HEAD_DIM:128])
        stk = jnp.concatenate([on[:, 0:qb], on[:, qb:2 * qb]], axis=0)
        o_ref[:, 128 * hk:128 * hk + 128] = stk.T.astype(BF16)


def _slc_call(u3, sb, bsz, seq):
    nslc = seq // NSA_SLC_L
    return pl.pallas_call(
        functools.partial(_slc_body, seq=seq),
        out_shape=jax.ShapeDtypeStruct((bsz, seq, GROUP_W), BF16),
        grid=(bsz, seq // SLC_QB),
        in_specs=[
            pl.BlockSpec((None, SLC_QB, GROUP_W), lambda b, i: (b, i, COL['b_q'] // GROUP_W)),
            pl.BlockSpec((None, 2, nslc, SLC_QB), lambda b, i: (b, 0, 0, i)),
            pl.BlockSpec((None, seq, 128), lambda b, i: (b, 0, COL['b_ks'] // 128)),
            pl.BlockSpec((None, seq, 128), lambda b, i: (b, 0, COL['b_vs'] // 128)),
        ],
        out_specs=pl.BlockSpec((None, SLC_QB, GROUP_W), lambda b, i: (b, i, 0)),
        scratch_shapes=[pltpu.VMEM((2, seq, 128), BF16), pltpu.VMEM((2, seq // SLC_KC, 128, SLC_KC), BF16),
                        pltpu.VMEM((2, 8, 2 * SLC_QB), F32), pltpu.VMEM((2, 128, 2 * SLC_QB), F32)],
        compiler_params=_params(2),
        name="nsa_slc",
    )(u3, sb, u3, u3).reshape(bsz * seq, GROUP_W)


GLA_TC = 512
_GLA_NLEV = 6


def _gla_constants():
    c = GLA_CHUNK
    idx = np.arange(c)
    t = idx[None, :]
    mats = [(t <= idx[:, None])]
    masks = []
    for lev in range(_GLA_NLEV):
        m = (c // 2) >> lev
        blk = idx // m
        ref_q = blk * m
        ref_k = (blk + 1) * m
        odd = (blk % 2 == 1)
        dq = (t > ref_q[:, None]) & (t <= idx[:, None]) & odd[:, None]
        dk = (t > idx[:, None]) & (t <= ref_k[:, None]) & (~odd)[:, None]
        mats.append(dq | dk)
        same_parent = (idx[:, None] // (2 * m)) == (idx[None, :] // (2 * m))
        masks.append(same_parent & odd[:, None] & (~odd)[None, :])
    masks.append(np.eye(c, dtype=bool))
    dall = np.concatenate(mats, axis=0).astype(np.float32)
    mask4 = np.stack([np.tile(mk.astype(np.float32), (1, N_HEADS)) for mk in masks])
    hv = np.arange(GROUP_W) // HEAD_DIM
    hd = np.arange(N_HEADS * GLA_DK) // GLA_DK
    bd = (hv[:, None] == hd[None, :]).astype(np.float32)
    same_head = (hv[:, None] == hv[None, :]).astype(np.float32)
    return dall, mask4, bd, same_head


def _gla_body(cq_ref, ck_ref, cv_ref, misc_ref, wa_ref, ba_ref, dall_ref, mask_ref, bd_ref,
              sh_ref, gn_ref, o_ref, st):
    j = pl.program_id(1)

    @pl.when(j == 0)
    def _():
        st[...] = jnp.zeros_like(st)

    c = GLA_CHUNK
    nk = N_HEADS * GLA_DK
    lane_h = lax.broadcasted_iota(jnp.int32, (1, nk), 1) // GLA_DK
    lane_v = lax.broadcasted_iota(jnp.int32, (1, GROUP_W), 1) // HEAD_DIM
    dall = dall_ref[...]
    nch = GLA_TC // c
    rows = [slice(ch * c, (ch + 1) * c) for ch in range(nch)]
    z = _dot(misc_ref[...], wa_ref[...]) + ba_ref[...]
    la = (jnp.minimum(z, 0.0) - jnp.log1p(jnp.exp(-jnp.abs(z)))) * (1.0 / GLA_TAU)
    hi, lo = _split_hi_lo(la)
    hl = jnp.concatenate([hi, lo], axis=1)
    e_all = []
    for r in rows:
        e2 = _dot(dall, hl[r])
        e_all.append(e2[:, 0:nk] + e2[:, nk:2 * nk])
    qf = [cq_ref[r, :].astype(F32) * (GLA_DK ** -0.5) for r in rows]
    kf = [ck_ref[r, :].astype(F32) for r in rows]
    vs = [cv_ref[r, :] for r in rows]
    atts, q_in, kv_new, decay = [], [], [], []
    for ch in range(nch):
        bcum = e_all[ch][0:c]
        att = jnp.zeros((c, N_HEADS * c), F32)
        for lev in range(_GLA_NLEV + 1):
            if lev < _GLA_NLEV:
                ex = jnp.exp(e_all[ch][(1 + lev) * c:(2 + lev) * c])
                ql, kl = qf[ch] * ex, kf[ch] * ex
            else:
                ql, kl = qf[ch], kf[ch]
            klb = kl.astype(BF16)
            kbd = jnp.concatenate([jnp.where(lane_h == h, klb, jnp.zeros_like(klb))
                                   for h in range(N_HEADS)], axis=0)
            att = att + _dot_nt(ql.astype(BF16), kbd) * mask_ref[lev]
        atts.append(att.astype(BF16))
        q_in.append((qf[ch] * jnp.exp(bcum)).astype(BF16))
        b_last = bcum[c - 1:c, :]
        kt = (kf[ch] * jnp.exp(b_last - bcum)).astype(BF16)
        kv_new.append(bd_ref[...] * _dot_tn(vs[ch], kt))
        decay.append(jnp.exp(b_last))
    intra = []
    for ch in range(nch):
        vbd = jnp.concatenate([jnp.where(lane_v == h, vs[ch], jnp.zeros_like(vs[ch]))
                               for h in range(N_HEADS)], axis=0)
        intra.append(_dot(atts[ch], vbd))
    state = st[...]
    outs = []
    for ch in range(nch):
        outs.append(intra[ch] + _dot_nt(q_in[ch], state.astype(BF16)))
        state = state * decay[ch] + kv_new[ch]
    st[...] = state
    for ch in range(nch):
        o = outs[ch]
        ms = _dot_hl(o * o, sh_ref[...]) * (1.0 / HEAD_DIM)
        o_ref[rows[ch], :] = (o * lax.rsqrt(ms + EPS) * gn_ref[...]).astype(BF16)


def _gla_call(u3, wa_emb, ba, gnorm, bsz, seq):
    dall, mask4, bd, same_head = _gla_constants()
    tc = GLA_TC
    full = lambda shape: pl.BlockSpec(shape, lambda b, j: (0,) * len(shape))
    return pl.pallas_call(
        _gla_body,
        out_shape=jax.ShapeDtypeStruct((bsz, seq, GROUP_W), BF16),
        grid=(bsz, seq // tc),
        in_specs=[
            pl.BlockSpec((None, tc, 128), lambda b, j: (b, j, COL['c_q'] // 128)),
            pl.BlockSpec((None, tc, 128), lambda b, j: (b, j, COL['c_k'] // 128)),
            pl.BlockSpec((None, tc, GROUP_W), lambda b, j: (b, j, COL['c_v'] // GROUP_W)),
            pl.BlockSpec((None, tc, 128), lambda b, j: (b, j, COL['misc'] // 128)),
            full((128, 128)), full((1, 128)), full(dall.shape), full(mask4.shape),
            full(bd.shape), full(same_head.shape), full((1, GROUP_W)),
        ],
        out_specs=pl.BlockSpec((None, tc, GROUP_W), lambda b, j: (b, j, 0)),
        scratch_shapes=[pltpu.VMEM((GROUP_W, N_HEADS * GLA_DK), F32)],
        compiler_params=_params(2),
        name="gla",
    )(u3, u3, u3, u3, wa_emb, ba, jnp.asarray(dall, BF16), jnp.asarray(mask4, F32),
      jnp.asarray(bd, F32), jnp.asarray(same_head, BF16), gnorm).reshape(bsz * seq, GROUP_W)


OUT_TM = 512


def _gather_residues(src_ref, dst, r, rows, width):
    for j in range(r):
        for s in range(width // 128):
            dst[s, pl.ds(j, rows, stride=r), :] = (
                src_ref[:, j * width + 128 * s:j * width + 128 * (s + 1)].astype(F32))


def _out_body(x_ref, mod_ref, ln_ref, z_ref, o1_ref, o2_ref, o3_ref, l1_ref, l2_ref, l3_ref,
              ocmp_ref, oslc_ref, owin_ref, oc_ref, od_ref, misc_ref, gb_ref, e4_ref, eg_ref,
              w_ref, out_ref, so2, sl2, so3, sl3):
    tm = OUT_TM
    _gather_residues(o2_ref, so2, 4, tm // 4, GROUP_W)
    _gather_residues(l2_ref, sl2, 4, tm // 4, 128)
    _gather_residues(o3_ref, so3, 16, tm // 16, GROUP_W)
    _gather_residues(l3_ref, sl3, 16, tm // 16, 128)
    l1, l2, l3 = l1_ref[...], sl2[0], sl3[0]
    mx = jnp.maximum(jnp.maximum(l1, l2), l3)
    e1, e2, e3 = jnp.exp(l1 - mx), jnp.exp(l2 - mx), jnp.exp(l3 - mx)
    inv = 1.0 / (e1 + e2 + e3)
    e4 = e4_ref[...]
    pair_out = (o1_ref[...].astype(F32),
                jnp.concatenate([so2[0], so2[1]], axis=1),
                jnp.concatenate([so3[0], so3[1]], axis=1))
    oa = jnp.zeros((OUT_TM, GROUP_W), F32)
    for e, o in zip((e1, e2, e3), pair_out):
        oa = oa + _dot_hl2(e * inv, e4) * o
    sg = jax.nn.sigmoid(misc_ref[...].astype(F32) + gb_ref[...])
    sg_hi, sg_lo = _split_hi_lo(sg)
    sg2 = jnp.concatenate([sg_hi, sg_lo], axis=1)
    ob = jnp.zeros((OUT_TM, GROUP_W), F32)
    for br, o_ref in enumerate((ocmp_ref, oslc_ref, owin_ref)):
        ob = ob + _dot(sg2, eg_ref[br]) * o_ref[...].astype(F32)
    z = z_ref[...].astype(F32)
    sz = z * jax.nn.sigmoid(z)
    mixed = jnp.concatenate([oa, ob, oc_ref[...].astype(F32), od_ref[...].astype(F32)], axis=1) * sz
    y = _dot(mixed.astype(BF16), w_ref[...])
    ms = jnp.mean(y * y, axis=-1, keepdims=True)
    yn = y * lax.rsqrt(ms + EPS) * ln_ref[...]
    out_ref[...] = x_ref[...] + mod_ref[2:3, :] * yn


def _out_call(x2, mod3, ln_post, u2, o1, o2, o3, l1, l2, l3, ocmp, oslc, owin, oc, od,
              gate_b_row, e4, eg, w_out, seq):
    t, d = x2.shape
    tm = OUT_TM
    per_b = seq // tm
    row = lambda w: pl.BlockSpec((tm, w), lambda i: (i, 0))
    full = lambda shape: pl.BlockSpec(shape, lambda i: (0,) * len(shape))
    grouped = lambda r, w: pl.BlockSpec((None, tm // r, r * w), lambda i: (i // per_b, i % per_b, 0))
    return pl.pallas_call(
        _out_body,
        out_shape=jax.ShapeDtypeStruct((t, d), F32),
        grid=(t // tm,),
        in_specs=[
            row(d),
            pl.BlockSpec((None, 3, d), lambda i: (i // per_b, 0, 0)),
            full((1, d)),
            pl.BlockSpec((tm, 1024), lambda i: (i, COL['z'] // 1024)),
            row(GROUP_W), grouped(4, GROUP_W), grouped(16, GROUP_W),
            row(128), grouped(4, 128), grouped(16, 128),
            row(GROUP_W), row(GROUP_W), row(GROUP_W), row(GROUP_W), row(GROUP_W),
            pl.BlockSpec((tm, 128), lambda i: (i, COL['misc'] // 128)),
            full((1, 128)), full((256, GROUP_W)), full((3, 256, GROUP_W)), full((d, d)),
        ],
        out_specs=row(d),
        scratch_shapes=[pltpu.VMEM((2, tm, 128), F32), pltpu.VMEM((1, tm, 128), F32),
                        pltpu.VMEM((2, tm, 128), F32), pltpu.VMEM((1, tm, 128), F32)],
        compiler_params=_params(1),
        name="mix_out",
    )(x2, mod3, ln_post.reshape(1, d), u2, o1, o2, o3, l1, l2, l3, ocmp, oslc, owin, oc, od,
      u2, gate_b_row, e4, eg, w_out)


def _permute_columns(w_in):
    pieces, start = [], 0
    src = _SRC_COLS.tolist()
    while start < len(src):
        end = start + 1
        while end < len(src) and ((src[start] < 0 and src[end] < 0) or
                                  (src[start] >= 0 and src[end] == src[end - 1] + 1)):
            end += 1
        if src[start] < 0:
            pieces.append(jnp.zeros(w_in.shape[:-1] + (end - start,), w_in.dtype))
        else:
            pieces.append(w_in[..., src[start]:src[end - 1] + 1])
        start = end
    return jnp.concatenate(pieces, axis=-1)


def _expand_w1(w1):
    w = w1.reshape(2, NSA_CMP_D, HEAD_DIM, NSA_CMP_HID)
    zero = jnp.zeros((NSA_CMP_D, HEAD_DIM, NSA_CMP_HID), w1.dtype)
    top = jnp.concatenate([w[0], w[1], zero, zero], axis=-1)
    bot = jnp.concatenate([zero, zero, w[0], w[1]], axis=-1)
    return jnp.concatenate([top, bot], axis=1).astype(BF16)


def _compress_weights(w1k, w1v):
    wk, wv = _expand_w1(w1k), _expand_w1(w1v)
    zero = jnp.zeros_like(wk)
    rows_k = jnp.concatenate([wk, zero], axis=-1)
    rows_v = jnp.concatenate([zero, wv], axis=-1)
    return jnp.concatenate([rows_k, rows_v], axis=1).reshape(NSA_CMP_D * C_W, 1024)


def _expand_w2(w2):
    z = jnp.zeros_like(w2)
    h0 = jnp.concatenate([w2, w2, z, z], axis=1)
    h1 = jnp.concatenate([z, z, w2, w2], axis=1)
    return jnp.stack([h0, h1]).astype(BF16)


def _static_tables(seq):
    ng = seq // NSA_CMP_D
    nslc = seq // NSA_SLC_L
    cs = np.arange(ng) * NSA_CMP_D
    ss = np.arange(nslc) * NSA_SLC_L
    ovt = ((cs[None, :] < ss[:, None] + NSA_SLC_L) & (cs[None, :] + NSA_CMP_L > ss[:, None]))
    ovt[:, ng - 1] = False
    e4 = np.zeros((128, GROUP_W), np.float32)
    eg = np.zeros((3, 128, GROUP_W), np.float32)
    for h in range(N_HEADS):
        e4[32 * h, HEAD_DIM * h:HEAD_DIM * (h + 1)] = 1.0
        for br in range(3):
            eg[br, MISC_G_OFF + 3 * h + br, HEAD_DIM * h:HEAD_DIM * (h + 1)] = 1.0
    e4 = np.concatenate([e4, e4], axis=0)
    eg = np.concatenate([eg, eg], axis=1)
    return jnp.asarray(ovt, BF16), jnp.asarray(e4, BF16), jnp.asarray(eg, BF16)


def kernel(x, c, ln_pre, ln_post, w_ada, b_ada, w_in, w_out, nsa_pos_k, nsa_pos_v, nsa_w1_k, nsa_b1_k, nsa_w2_k, nsa_w1_v, nsa_b1_v, nsa_w2_v, nsa_gate_b, gla_w_a2, gla_b_a, gla_norm, sinks):
    bsz, seq, d = x.shape
    depth = w_in.shape[0]
    t = bsz * seq

    w_in_r = _permute_columns(w_in).astype(BF16)
    w_out_b = w_out.astype(BF16)
    ovt, e4, eg = _static_tables(seq)

    mod = _ada_call(c, w_ada, b_ada).reshape(depth, bsz, 3, d)

    x2 = x.reshape(t, d)
    for i in range(depth):
        u2, ua, ua4, ua16, uc16 = _proj_in_call(x2, mod[i], ln_pre[i], w_in_r[i], bsz, seq)
        u3 = u2.reshape(bsz, seq, U_W)

        grouped_a = {1: ua.reshape(bsz, seq, A_W), 4: ua4, 16: ua16}
        dil = [_dil_pair_call(grouped_a[r], bsz, seq, w, r) for w, r in DIL_PAIRS]

        kc_rep, vc_rep = _compress_call(
            uc16, bsz, seq, _compress_weights(nsa_w1_k[i], nsa_w1_v[i]),
            jnp.broadcast_to(nsa_pos_k[i].reshape(1, -1), (8, NSA_CMP_L * HEAD_DIM)).astype(BF16),
            jnp.broadcast_to(nsa_pos_v[i].reshape(1, -1), (8, NSA_CMP_L * HEAD_DIM)).astype(BF16),
            nsa_w1_k[i].astype(BF16), nsa_w1_v[i].astype(BF16),
            nsa_b1_k[i].reshape(1, -1), nsa_b1_v[i].reshape(1, -1),
            _expand_w2(nsa_w2_k[i]), _expand_w2(nsa_w2_v[i]))
        ocmp, sb = _cmp_sel_call(u3, kc_rep, vc_rep, ovt, bsz, seq)
        oslc = _slc_call(u3, sb, bsz, seq)
        owin = _win_call(u3, bsz, seq)

        wa_emb = jnp.zeros((128, 128), F32).at[MISC_A_OFF:MISC_A_OFF + GLA_RANK, :].set(gla_w_a2[i]).astype(BF16)
        oc = _gla_call(u3, wa_emb, gla_b_a[i].reshape(1, -1), gla_norm[i].reshape(1, -1), bsz, seq)

        od = _swa_call(u3, sinks[i], bsz, seq)

        gate_b_row = jnp.zeros((1, 128), F32).at[0, MISC_G_OFF:MISC_G_OFF + 3 * N_HEADS].set(nsa_gate_b[i])
        x2 = _out_call(x2, mod[i], ln_post[i], u2, dil[0][0].reshape(t, GROUP_W), dil[1][0], dil[2][0],
                       dil[0][1].reshape(t, 128), dil[1][1], dil[2][1], ocmp.reshape(t, GROUP_W), oslc, owin, oc, od,
                       gate_b_row, e4, eg, w_out_b[i], seq)
    return x2.reshape(bsz, seq, d)
```

```python
import functools

import numpy as np
import jax
import jax.numpy as jnp
from jax import lax
from jax.experimental import pallas as pl
from jax.experimental.pallas import tpu as pltpu

F32 = jnp.float32
BF16 = jnp.bfloat16

D_MODEL = 1024
HEAD_DIM = 64
N_HEADS = 4
GROUP_W = N_HEADS * HEAD_DIM
EPS = 1e-6
NEG = -1e30
QBLK = 128

DIL_PAIRS = ((128, 1), (512, 4), (2048, 16))
NSA_CMP_L = 32
NSA_CMP_D = 16
NSA_CMP_HID = 128
NSA_SLC_L = 64
NSA_TOP_N = 16
NSA_WIN = 512
GLA_DK = 32
GLA_RANK = 16
GLA_TAU = 16.0
GLA_CHUNK = 64
SWA_WIN = 128

VMEM_LIMIT = 56 * 1024 * 1024

_ORIG_SPLITS = (
    ('a_q', 256), ('a_k', 256), ('a_v', 256), ('a_z', 256),
    ('b_q', 256), ('b_kc', 128), ('b_vc', 128), ('b_ks', 128), ('b_vs', 128),
    ('b_kw', 128), ('b_vw', 128), ('b_g', 12), ('b_z', 256),
    ('c_q', 128), ('c_k', 128), ('c_v', 256), ('c_a', 16), ('c_z', 256),
    ('d_q', 256), ('d_k', 64), ('d_v', 64), ('d_z', 256),
)
_NEW_ORDER = (
    ('a_q', ('a_q',), 256), ('a_k', ('a_k',), 256), ('a_v', ('a_v',), 256),
    ('b_kc', ('b_kc',), 128), ('b_vc', ('b_vc',), 128),
    ('z', ('a_z', 'b_z', 'c_z', 'd_z'), 1024),
    ('b_q', ('b_q',), 256), ('d_q', ('d_q',), 256), ('c_v', ('c_v',), 256),
    ('b_ks', ('b_ks',), 128), ('b_vs', ('b_vs',), 128), ('b_kw', ('b_kw',), 128),
    ('b_vw', ('b_vw',), 128), ('c_q', ('c_q',), 128), ('c_k', ('c_k',), 128),
    ('d_kv', ('d_k', 'd_v'), 128), ('misc', ('b_g', 'c_a'), 128),
)
A_W = 768
C_W = 256
MISC_G_OFF = 0
MISC_A_OFF = 12


def _build_layout():
    off, o = {}, 0
    for name, w in _ORIG_SPLITS:
        off[name] = (o, w)
        o += w
    src, col, n = [], {}, 0
    for name, parts, width in _NEW_ORDER:
        col[name] = n
        used = 0
        for p in parts:
            s, w = off[p]
            src.extend(range(s, s + w))
            used += w
        src.extend([-1] * (width - used))
        n += width
    return np.asarray(src, np.int32), col, n


_SRC_COLS, COL, U_W = _build_layout()
assert COL['a_q'] == 0 and COL['b_kc'] == A_W and COL['z'] == A_W + C_W


def _dot(a, b):
    return jnp.dot(a, b, preferred_element_type=F32)


def _dot_nt(a, b):
    return lax.dot_general(a, b, (((1,), (1,)), ((), ())), preferred_element_type=F32)


def _dot_tn(a, b):
    return lax.dot_general(a, b, (((0,), (0,)), ((), ())), preferred_element_type=F32)


def _split_hi_lo(x):
    hi = x.astype(BF16)
    lo = (x - hi.astype(F32)).astype(BF16)
    return hi, lo


def _dot_hl(x, w):
    hi, lo = _split_hi_lo(x)
    return _dot(hi, w) + _dot(lo, w)


def _dot_hl2(x, w2):
    hi, lo = _split_hi_lo(x)
    return _dot(jnp.concatenate([hi, lo], axis=1), w2)


def _params(n_axes):
    return pltpu.CompilerParams(dimension_semantics=("arbitrary",) * n_axes,
                                vmem_limit_bytes=VMEM_LIMIT)


def _ada_body(c_ref, w_ref, b_ref, o_ref):
    c = c_ref[...]
    sc = (c * jax.nn.sigmoid(c)).astype(BF16)
    o_ref[...] = _dot(sc, w_ref[...].astype(BF16)) + b_ref[...]


def _ada_call(c, w_ada, b_ada):
    depth, d, d3 = w_ada.shape
    b = c.shape[0]
    nb = d3 // d
    return pl.pallas_call(
        _ada_body,
        out_shape=jax.ShapeDtypeStruct((depth, b, d3), F32),
        grid=(depth, nb),
        in_specs=[
            pl.BlockSpec((b, d), lambda i, n: (0, 0)),
            pl.BlockSpec((None, d, d), lambda i, n: (i, 0, n)),
            pl.BlockSpec((None, 1, d), lambda i, n: (i, 0, n)),
        ],
        out_specs=pl.BlockSpec((None, b, d), lambda i, n: (i, 0, n)),
        compiler_params=_params(2),
        name="ada_mod",
    )(c, w_ada, b_ada.reshape(depth, 1, d3))


PROJ_TM = 512
_PROJ_CHUNKS = ((0, 768), (768, 1024), (1024, 1792), (1792, 2560), (2560, 3328), (3328, 3840))


def _scatter_4_then_16(src, dst4_ref, dst16_ref, mid, rows4, width):
    ns = width // 128
    for j4 in range(4):
        for s in range(ns):
            v = src[s, pl.ds(j4, rows4, stride=4), :]
            if dst4_ref is not None:
                dst4_ref[:, j4 * width + 128 * s:j4 * width + 128 * (s + 1)] = v.astype(dst4_ref.dtype)
            mid[j4 * ns + s] = v
    for j4 in range(4):
        for e in range(4):
            for s in range(ns):
                c0 = (4 * e + j4) * width + 128 * s
                dst16_ref[:, c0:c0 + 128] = mid[j4 * ns + s, pl.ds(e, rows4 // 4, stride=4), :].astype(dst16_ref.dtype)


def _proj_in_body(x_ref, mod_ref, ln_ref, w_ref, o_ref, oa_ref, oa4_ref, oa16_ref, oc16_ref, sa, sc, mida, midc):
    x = x_ref[...]
    ms = jnp.mean(x * x, axis=-1, keepdims=True)
    y = x * lax.rsqrt(ms + EPS) * ln_ref[...]
    shift = mod_ref[0:1, :]
    scale = mod_ref[1:2, :]
    h = (y * (1.0 + scale) + shift).astype(BF16)
    tm = PROJ_TM
    for n0, n1 in _PROJ_CHUNKS:
        res = _dot(h, w_ref[:, n0:n1])
        r = res.astype(BF16)
        o_ref[:, n0:n1] = r
        if (n0, n1) == (0, A_W):
            oa_ref[...] = r
            for s in range(A_W // 128):
                sa[s] = res[:, 128 * s:128 * (s + 1)]
            _scatter_4_then_16(sa, oa4_ref, oa16_ref, mida, tm // 4, A_W)
        elif (n0, n1) == (A_W, A_W + C_W):
            for s in range(C_W // 128):
                sc[s] = res[:, 128 * s:128 * (s + 1)]
            _scatter_4_then_16(sc, None, oc16_ref, midc, tm // 4, C_W)


def _proj_in_call(x2, mod3, ln_pre, w_r, bsz, seq):
    t, d = x2.shape
    tm = PROJ_TM
    per_b = seq // tm
    grouped = lambda r, w: pl.BlockSpec((None, tm // r, r * w), lambda i: (i // per_b, i % per_b, 0))
    return pl.pallas_call(
        _proj_in_body,
        out_shape=(jax.ShapeDtypeStruct((t, U_W), BF16),
                   jax.ShapeDtypeStruct((t, A_W), BF16),
                   jax.ShapeDtypeStruct((bsz, seq // 4, 4 * A_W), BF16),
                   jax.ShapeDtypeStruct((bsz, seq // 16, 16 * A_W), BF16),
                   jax.ShapeDtypeStruct((bsz, seq // NSA_CMP_D, NSA_CMP_D * C_W), BF16)),
        grid=(t // tm,),
        in_specs=[
            pl.BlockSpec((tm, d), lambda i: (i, 0)),
            pl.BlockSpec((None, 3, d), lambda i: (i // per_b, 0, 0)),
            pl.BlockSpec((1, d), lambda i: (0, 0)),
            pl.BlockSpec((d, U_W), lambda i: (0, 0)),
        ],
        out_specs=(pl.BlockSpec((tm, U_W), lambda i: (i, 0)),
                   pl.BlockSpec((tm, A_W), lambda i: (i, 0)),
                   grouped(4, A_W), grouped(16, A_W), grouped(NSA_CMP_D, C_W)),
        scratch_shapes=[pltpu.VMEM((A_W // 128, tm, 128), F32), pltpu.VMEM((C_W // 128, tm, 128), F32),
                        pltpu.VMEM((4 * A_W // 128, tm // 4, 128), F32),
                        pltpu.VMEM((4 * C_W // 128, tm // 4, 128), F32)],
        compiler_params=_params(1),
        name="proj_in",
    )(x2, mod3, ln_pre.reshape(1, d), w_r)


def _stack_heads(q):
    lane_h = lax.broadcasted_iota(jnp.int32, (1, GROUP_W), 1) // HEAD_DIM
    zero = jnp.zeros_like(q)
    return jnp.concatenate([jnp.where(lane_h == h, q, zero) for h in range(N_HEADS)], axis=0)


def _pick_heads(r):
    lane_h = lax.broadcasted_iota(jnp.int32, (1, GROUP_W), 1) // HEAD_DIM
    out = jnp.where(lane_h == 0, r[0], 0.0)
    for h in range(1, N_HEADS):
        out = jnp.where(lane_h == h, r[h], out)
    return out


def _band_bias_table(blk, span, nb, back):
    d = np.arange(nb + 1)[:, None, None] * blk + np.arange(blk)[None, :, None] - np.arange(span)[None, None, :]
    return jnp.asarray(np.where((d >= 0) & (d <= back), 0.0, NEG), F32)


def _banded_core(i, q_ref, k_src, v_src, bias_ref, o_ref, lse_ref, sink_ref, *, blk, nsub, span, nb, back):
    if sink_ref is not None:
        hidx = lax.broadcasted_iota(jnp.int32, (N_HEADS, 1, 1), 0)
        sink = jnp.zeros((N_HEADS, 1, 1), F32)
        for h in range(N_HEADS):
            sink = jnp.where(hidx == h, sink_ref[h], sink)
    lane_l = lax.broadcasted_iota(jnp.int32, (1, 128), 1) // 32

    def window(sb):
        ib = i * nsub + sb
        return ib, pl.multiple_of(jnp.maximum(ib - nb, 0) * blk, blk)

    def qk(sb):
        ib, start = window(sb)
        q = q_ref[sb * blk:(sb + 1) * blk, :] * jnp.asarray(HEAD_DIM ** -0.5, BF16)
        s = _dot_nt(_stack_heads(q), k_src[pl.ds(start, span), :]).reshape(N_HEADS, blk, span)
        return s + bias_ref[jnp.minimum(ib, nb)][None]

    s_next = qk(0)
    for sb in range(nsub):
        s = s_next
        s_next = qk(sb + 1) if sb + 1 < nsub else None
        vb = v_src[pl.ds(window(sb)[1], span), :]
        m = jnp.max(s, axis=-1, keepdims=True)
        if sink_ref is not None:
            m = jnp.maximum(m, sink)
        p = jnp.exp(s - m)
        l = jnp.sum(p, axis=-1, keepdims=True)
        den = l if sink_ref is None else l + jnp.exp(sink - m)
        r = _dot(p.reshape(N_HEADS * blk, span).astype(BF16), vb).reshape(N_HEADS, blk, GROUP_W)
        o_ref[sb * blk:(sb + 1) * blk, :] = _pick_heads(r * (1.0 / den)).astype(o_ref.dtype)
        if lse_ref is not None:
            lse = m + jnp.log(l)
            acc = jnp.zeros((blk, 128), F32)
            for h in range(N_HEADS):
                acc = jnp.where(lane_l == h, lse[h], acc)
            lse_ref[sb * blk:(sb + 1) * blk, :] = acc


def _dil_body(q_ref, k_ref, v_ref, bias_ref, o_ref, lse_ref, **kw):
    _banded_core(pl.program_id(2), q_ref, k_ref, v_ref, bias_ref, o_ref, lse_ref, None, **kw)


def _dil_multi_body(qkv_ref, bias_ref, o_ref, lse_ref, *, nres, **kw):
    for jj in range(nres):
        c0 = jj * A_W
        _banded_core(0, qkv_ref.at[:, c0:c0 + GROUP_W], qkv_ref.at[:, c0 + GROUP_W:c0 + 2 * GROUP_W],
                     qkv_ref.at[:, c0 + 2 * GROUP_W:c0 + 3 * GROUP_W], bias_ref,
                     o_ref.at[:, jj * GROUP_W:(jj + 1) * GROUP_W], lse_ref.at[:, jj * 128:(jj + 1) * 128],
                     None, **kw)


DIL_RES_PER_STEP = 8
BAND_MAX_SUB = 8


def _banded_geometry(ln, back):
    blk = min(QBLK, ln)
    nb = -(-back // blk)
    span = min((nb + 1) * blk, ln)
    nsub = next(n for n in (BAND_MAX_SUB, 4, 2, 1) if ln % (n * blk) == 0)
    return blk, nb, span, nsub


def _dil_pair_call(uv, bsz, seq, w, r):
    ln = seq // r
    back = w // r
    blk, nb, span, nsub = _banded_geometry(ln, back)
    rows = blk * nsub
    cpb = A_W // GROUP_W
    nres = min(DIL_RES_PER_STEP, r)
    if rows == ln and r % nres == 0:
        return pl.pallas_call(
            functools.partial(_dil_multi_body, nres=nres, blk=blk, nsub=nsub, span=span, nb=nb, back=back),
            out_shape=(jax.ShapeDtypeStruct((bsz, ln, r * GROUP_W), BF16),
                       jax.ShapeDtypeStruct((bsz, ln, r * 128), F32)),
            grid=(bsz, r // nres),
            in_specs=[
                pl.BlockSpec((None, ln, nres * A_W), lambda b, j: (b, 0, j)),
                pl.BlockSpec((nb + 1, blk, span), lambda b, j: (0, 0, 0)),
            ],
            out_specs=(pl.BlockSpec((None, ln, nres * GROUP_W), lambda b, j: (b, 0, j)),
                       pl.BlockSpec((None, ln, nres * 128), lambda b, j: (b, 0, j))),
            compiler_params=_params(2),
            name=f"dil_r{r}",
        )(uv, _band_bias_table(blk, span, nb, back))
    o, lse = pl.pallas_call(
        functools.partial(_dil_body, blk=blk, nsub=nsub, span=span, nb=nb, back=back),
        out_shape=(jax.ShapeDtypeStruct((bsz, ln, r * GROUP_W), BF16),
                   jax.ShapeDtypeStruct((bsz, ln, r * 128), F32)),
        grid=(bsz, r, ln // rows),
        in_specs=[
            pl.BlockSpec((None, rows, GROUP_W), lambda b, j, i: (b, i, j * cpb)),
            pl.BlockSpec((None, ln, GROUP_W), lambda b, j, i: (b, 0, j * cpb + 1)),
            pl.BlockSpec((None, ln, GROUP_W), lambda b, j, i: (b, 0, j * cpb + 2)),
            pl.BlockSpec((nb + 1, blk, span), lambda b, j, i: (0, 0, 0)),
        ],
        out_specs=(pl.BlockSpec((None, rows, GROUP_W), lambda b, j, i: (b, i, j)),
                   pl.BlockSpec((None, rows, 128), lambda b, j, i: (b, i, j))),
        compiler_params=_params(3),
        name=f"dil_r{r}",
    )(uv, uv, uv, _band_bias_table(blk, span, nb, back))
    return o, lse


def _rep_prep(src_refs, dst_refs, seq, mode):
    ch = 512
    lane = lax.broadcasted_iota(jnp.int32, (ch, 128), 1)
    lo = lane < HEAD_DIM
    for c in range(seq // ch):
        rows = slice(c * ch, (c + 1) * ch)
        if mode == 'swa':
            x = src_refs[0][rows, :].astype(F32)
            sw = pltpu.roll(x, HEAD_DIM, 1)
            kk = jnp.where(lo, x, sw).astype(BF16)
            vv = jnp.where(lo, sw, x).astype(BF16)
            dst_refs[0][rows, 0:128] = kk
            dst_refs[0][rows, 128:256] = kk
            dst_refs[1][rows, 0:128] = vv
            dst_refs[1][rows, 128:256] = vv
        else:
            for s_ref, d_ref in zip(src_refs, dst_refs):
                x = s_ref[rows, :].astype(F32)
                sw = pltpu.roll(x, HEAD_DIM, 1)
                d_ref[rows, 0:128] = jnp.where(lo, x, sw).astype(BF16)
                d_ref[rows, 128:256] = jnp.where(lo, sw, x).astype(BF16)


def _swa_body(sink_ref, q_ref, kv_ref, bias_ref, o_ref, kr, vr, *, seq, **kw):
    i = pl.program_id(1)

    @pl.when(i == 0)
    def _():
        _rep_prep((kv_ref,), (kr, vr), seq, 'swa')

    _banded_core(i, q_ref, kr, vr, bias_ref, o_ref, None, sink_ref, **kw)


def _win_body(q_ref, kw_ref, vw_ref, bias_ref, o_ref, kr, vr, *, seq, **kw):
    i = pl.program_id(1)

    @pl.when(i == 0)
    def _():
        _rep_prep((kw_ref, vw_ref), (kr, vr), seq, 'win')

    _banded_core(i, q_ref, kr, vr, bias_ref, o_ref, None, None, **kw)


def _shared_kv_call(name, body, operands, specs, q_col, bsz, seq, back):
    blk, nb, span, nsub = _banded_geometry(seq, back)
    rows = blk * nsub
    return pl.pallas_call(
        functools.partial(body, seq=seq, blk=blk, nsub=nsub, span=span, nb=nb, back=back),
        out_shape=jax.ShapeDtypeStruct((bsz, seq, GROUP_W), BF16),
        grid=(bsz, seq // rows),
        in_specs=(specs[0] + [pl.BlockSpec((None, rows, GROUP_W), lambda b, i: (b, i, q_col // GROUP_W))]
                  + specs[1] + [pl.BlockSpec((nb + 1, blk, span), lambda b, i: (0, 0, 0))]),
        out_specs=pl.BlockSpec((None, rows, GROUP_W), lambda b, i: (b, i, 0)),
        scratch_shapes=[pltpu.VMEM((seq, GROUP_W), BF16), pltpu.VMEM((seq, GROUP_W), BF16)],
        compiler_params=_params(2),
        name=name,
    )(*operands, _band_bias_table(blk, span, nb, back)).reshape(bsz * seq, GROUP_W)


def _swa_call(u3, sinks, bsz, seq):
    kv = lambda col: pl.BlockSpec((None, seq, 128), lambda b, i: (b, 0, col // 128))
    return _shared_kv_call("swa", _swa_body, (sinks, u3, u3),
                           ([pl.BlockSpec(memory_space=pltpu.SMEM)], [kv(COL['d_kv'])]),
                           COL['d_q'], bsz, seq, SWA_WIN - 1)


def _win_call(u3, bsz, seq):
    kv = lambda col: pl.BlockSpec((None, seq, 128), lambda b, i: (b, 0, col // 128))
    return _shared_kv_call("nsa_win", _win_body, (u3, u3, u3),
                           ([], [kv(COL['b_kw']), kv(COL['b_vw'])]),
                           COL['b_q'], bsz, seq, NSA_WIN - 1)


def _compress_body(x_ref, w_ref, pk_ref, pv_ref, w1k_ref, w1v_ref,
                   b1k_ref, b1v_ref, w2k_ref, w2v_ref, ok_ref, ov_ref, *, ng):
    acc_kv = _dot(x_ref[...], w_ref[...])
    for off, p_ref, w1_ref, b1_ref, w2_ref, o_ref in (
            (0, pk_ref, w1k_ref, b1k_ref, w2k_ref, ok_ref),
            (512, pv_ref, w1v_ref, b1v_ref, w2v_ref, ov_ref)):
        const = _dot(p_ref[...], w1_ref[...])[0:1, :] + b1_ref[...]
        out = jnp.zeros((ng, GROUP_W), F32)
        for h in range(2):
            first = acc_kv[:, off + 256 * h:off + 256 * h + 128]
            second = acc_kv[:, off + 256 * h + 128:off + 256 * h + 256]
            pre = first + pltpu.roll(second, ng - 1, 0) + const
            act = jax.nn.gelu(pre, approximate=True)
            out = out + _dot(act.astype(BF16), w2_ref[h])
        o_ref[...] = out.astype(BF16)


def _compress_call(uc16, bsz, seq, w_all, posk, posv, w1k, w1v, b1k, b1v, w2k_exp, w2v_exp):
    ng = seq // NSA_CMP_D
    full = lambda shape: pl.BlockSpec(shape, lambda b: (0,) * len(shape))
    return pl.pallas_call(
        functools.partial(_compress_body, ng=ng),
        out_shape=(jax.ShapeDtypeStruct((bsz, ng, GROUP_W), BF16),
                   jax.ShapeDtypeStruct((bsz, ng, GROUP_W), BF16)),
        grid=(bsz,),
        in_specs=[
            pl.BlockSpec((None, ng, NSA_CMP_D * C_W), lambda b: (b, 0, 0)),
            full((NSA_CMP_D * C_W, 1024)),
            full((8, NSA_CMP_L * HEAD_DIM)), full((8, NSA_CMP_L * HEAD_DIM)),
            full((NSA_CMP_L * HEAD_DIM, NSA_CMP_HID)), full((NSA_CMP_L * HEAD_DIM, NSA_CMP_HID)),
            full((1, NSA_CMP_HID)), full((1, NSA_CMP_HID)),
            full((2, NSA_CMP_HID, GROUP_W)), full((2, NSA_CMP_HID, GROUP_W)),
        ],
        out_specs=(pl.BlockSpec((None, ng, GROUP_W), lambda b: (b, 0, 0)),
                   pl.BlockSpec((None, ng, GROUP_W), lambda b: (b, 0, 0))),
        compiler_params=_params(1),
        name="nsa_compress",
    )(uc16, w_all, posk, posv, w1k, w1v, b1k, b1v, w2k_exp, w2v_exp)


CMP_NSB = 4


def _cmp_sel_body(q_ref, kc_ref, vc_ref, ovt_ref, o_ref, sb_ref, scr, *, ng, nslc):
    i = pl.program_id(1)
    q0 = i * (CMP_NSB * QBLK)
    ci = lax.broadcasted_iota(jnp.int32, (QBLK, ng), 1)
    trow = lax.broadcasted_iota(jnp.int32, (QBLK, ng), 0)
    jrow = lax.broadcasted_iota(jnp.int32, (nslc, QBLK), 0)
    qcol = lax.broadcasted_iota(jnp.int32, (nslc, QBLK), 1)
    ovt = ovt_ref[...]

    def qk(sb):
        q = q_ref[sb * QBLK:(sb + 1) * QBLK, :] * jnp.asarray(HEAD_DIM ** -0.5, BF16)
        return _dot_nt(_stack_heads(q), kc_ref[...]).reshape(N_HEADS, QBLK, ng)

    forced, causal = [], []
    s_next = qk(0)
    for sb in range(CMP_NSB):
        s_raw = s_next
        s_next = qk(sb + 1) if sb + 1 < CMP_NSB else None
        qs = q0 + sb * QBLK
        valid = ((ci * NSA_CMP_D + (NSA_CMP_L - 1)) <= qs + trow)[None]
        s = jnp.where(valid, s_raw, NEG)
        m = jnp.max(s, axis=-1, keepdims=True)
        e = jnp.where(valid, jnp.exp(s - m), 0.0)
        l = jnp.sum(e, axis=-1, keepdims=True)
        p = e * (1.0 / jnp.maximum(l, 1e-30))
        r = _dot(p.reshape(N_HEADS * QBLK, ng).astype(BF16), vc_ref[...]).reshape(N_HEADS, QBLK, GROUP_W)
        o_ref[sb * QBLK:(sb + 1) * QBLK, :] = _pick_heads(r).astype(BF16)

        cur = (qs + qcol) // NSA_SLC_L
        forced.append(jnp.where(jrow == 0, 1.0, jnp.where(jrow == cur, 1.0, jnp.where(jrow == cur - 1, 1.0, 0.0))))
        causal.append(jnp.where(jrow <= cur, 1.0, 0.0))
        for hk in range(2):
            psum = p[2 * hk] + p[2 * hk + 1]
            hi, lo = _split_hi_lo(psum)
            imp_t = _dot_nt(ovt, hi) + _dot_nt(ovt, lo)
            scr[2 * sb + hk] = jnp.where(causal[sb] * (1.0 - forced[sb]) > 0.5, imp_t, -1.0)

    n_free = NSA_TOP_N - 3
    nv = nslc // 8
    sub = lax.broadcasted_iota(jnp.int32, (8, QBLK), 0)
    tier = (q0 + CMP_NSB * QBLK - 1) // (8 * NSA_SLC_L)
    for tt in range(nv):
        @pl.when(tier == tt)
        def _(tt=tt):
            nvt = tt + 1
            for sb in range(CMP_NSB):
                for hk in range(2):
                    k = 2 * sb + hk
                    vals = [scr[k, 8 * v:8 * v + 8, :] for v in range(nvt)]
                    cnts = [jnp.zeros((8, QBLK), F32) for _ in range(nvt)]
                    for jp in range(8 * nvt):
                        rv = jnp.broadcast_to(scr[k, jp:jp + 1, :], (8, QBLK))
                        for v in range(nvt):
                            if 8 * v > jp:
                                beats = jnp.where(rv >= vals[v], 1.0, 0.0)
                            elif 8 * v + 7 <= jp:
                                beats = jnp.where(rv > vals[v], 1.0, 0.0)
                            else:
                                tie = jnp.where(sub + 8 * v > jp, jnp.where(rv == vals[v], 1.0, 0.0), 0.0)
                                beats = jnp.where(rv > vals[v], 1.0, 0.0) + tie
                            cnts[v] = cnts[v] + beats
                    pieces = [jnp.where(c < n_free - 0.5, 1.0, 0.0) for c in cnts]
                    if nvt < nv:
                        pieces.append(jnp.zeros((8 * (nv - nvt), QBLK), F32))
                    picked = jnp.concatenate(pieces, axis=0)
                    sel_t = causal[sb] * jnp.maximum(forced[sb], picked)
                    sb_ref[hk, :, sb * QBLK:(sb + 1) * QBLK] = jnp.where(sel_t > 0.5, 0.0, NEG).astype(BF16)


def _cmp_sel_call(u3, kc_rep, vc_rep, ovt, bsz, seq):
    ng = seq // NSA_CMP_D
    nslc = seq // NSA_SLC_L
    assert nslc == 64, "selection-bias layout assumes 64 selection blocks"
    qb = CMP_NSB * QBLK
    return pl.pallas_call(
        functools.partial(_cmp_sel_body, ng=ng, nslc=nslc),
        out_shape=(jax.ShapeDtypeStruct((bsz, seq, GROUP_W), BF16),
                   jax.ShapeDtypeStruct((bsz, 2, nslc, seq), BF16)),
        grid=(bsz, seq // qb),
        in_specs=[
            pl.BlockSpec((None, qb, GROUP_W), lambda b, i: (b, i, COL['b_q'] // GROUP_W)),
            pl.BlockSpec((None, ng, GROUP_W), lambda b, i: (b, 0, 0)),
            pl.BlockSpec((None, ng, GROUP_W), lambda b, i: (b, 0, 0)),
            pl.BlockSpec((nslc, ng), lambda b, i: (0, 0)),
        ],
        out_specs=(pl.BlockSpec((None, qb, GROUP_W), lambda b, i: (b, i, 0)),
                   pl.BlockSpec((None, 2, nslc, qb), lambda b, i: (b, 0, 0, i))),
        scratch_shapes=[pltpu.VMEM((2 * CMP_NSB, nslc, QBLK), F32)],
        compiler_params=_params(2),
        name="nsa_cmp_sel",
    )(u3, kc_rep, vc_rep, ovt)


SLC_KC = 512
SLC_QB = 256


def _slc_body(q_ref, sb_ref, ks_ref, vs_ref, o_ref, kaug, vaugt, m_s, acc_s, *, seq):
    i = pl.program_id(1)
    qb = SLC_QB
    q0 = i * qb
    kc = SLC_KC

    @pl.when(i == 0)
    def _():
        ch = 128
        lane = lax.broadcasted_iota(jnp.int32, (ch, 128), 1)
        lo = lane < HEAD_DIM
        top = lax.broadcasted_iota(jnp.int32, (128, ch), 0) < HEAD_DIM
        for c in range(seq // ch):
            rows = slice(c * ch, (c + 1) * ch)
            r = c * ch + lax.broadcasted_iota(jnp.int32, (ch, 128), 0)
            onehot = jnp.where((lane & (HEAD_DIM - 1)) == r // NSA_SLC_L, 1.0, 0.0)
            kf = ks_ref[rows, :].astype(F32)
            kaug[0, rows, :] = jnp.where(lo, kf, onehot).astype(BF16)
            kaug[1, rows, :] = jnp.where(lo, pltpu.roll(kf, HEAD_DIM, 1), onehot).astype(BF16)
            vt = vs_ref[rows, :].astype(F32).T
            cols = slice((c * ch) % kc, (c * ch) % kc + ch)
            vaugt[0, (c * ch) // kc, :, cols] = jnp.where(top, vt, 1.0).astype(BF16)
            vaugt[1, (c * ch) // kc, :, cols] = jnp.where(top, pltpu.roll(vt, HEAD_DIM, 0), 1.0).astype(BF16)

    qf = q_ref[...].astype(F32) * (HEAD_DIM ** -0.5)
    ws = []
    for hk in range(2):
        qt = qf[:, 128 * hk:128 * hk + 128].T.astype(BF16)
        bt = sb_ref[hk]
        ws.append(jnp.concatenate([jnp.concatenate([qt[0:HEAD_DIM], bt], axis=0),
                                   jnp.concatenate([qt[HEAD_DIM:128], bt], axis=0)], axis=1))

    m_s[...] = jnp.full(m_s.shape, NEG, F32)
    acc_s[...] = jnp.zeros(acc_s.shape, F32)

    rel = (lax.broadcasted_iota(jnp.int32, (kc, 2 * qb), 0)
           - (lax.broadcasted_iota(jnp.int32, (kc, 2 * qb), 1) & (qb - 1)))

    def scores(c, masked):
        out = []
        for hk in range(2):
            st = _dot(kaug[hk, c * kc:(c + 1) * kc, :], ws[hk])
            if masked:
                st = jnp.where(rel <= q0 - c * kc, st, NEG)
            out.append((st, jnp.max(st, axis=0, keepdims=True)))
        return out

    def consume(c, sm):
        m_old = [m_s[hk, 0:1, :] for hk in range(2)]
        m_new = [jnp.maximum(m_old[hk], sm[hk][1]) for hk in range(2)]
        pts = [jnp.exp(sm[hk][0] - m_new[hk]).astype(BF16) for hk in range(2)]
        pvs = [_dot(vaugt[hk, c], pts[hk]) for hk in range(2)]
        for hk in range(2):
            acc_s[hk] = acc_s[hk] * jnp.exp(m_old[hk] - m_new[hk]) + pvs[hk]
            m_s[hk] = jnp.broadcast_to(m_new[hk], (8, 2 * qb))

    n_last = (q0 + qb - 1) // kc

    def run(n_chunks):
        sm = scores(0, n_chunks == 1)
        for c in range(n_chunks):
            nxt = scores(c + 1, c + 2 == n_chunks) if c + 1 < n_chunks else None
            consume(c, sm)
            sm = nxt

    for n in range(seq // kc):
        pl.when(n_last == n)(functools.partial(run, n + 1))

    for hk in range(2):
        a = acc_s[hk]
        on = a[0:HEAD_DIM] * (1.0 / a[HEAD_DIM:128])
        stk = jnp.concatenate([on[:, 0:qb], on[:, qb:2 * qb]], axis=0)
        o_ref[:, 128 * hk:128 * hk + 128] = stk.T.astype(BF16)


def _slc_call(u3, sb, bsz, seq):
    nslc = seq // NSA_SLC_L
    return pl.pallas_call(
        functools.partial(_slc_body, seq=seq),
        out_shape=jax.ShapeDtypeStruct((bsz, seq, GROUP_W), BF16),
        grid=(bsz, seq // SLC_QB),
        in_specs=[
            pl.BlockSpec((None, SLC_QB, GROUP_W), lambda b, i: (b, i, COL['b_q'] // GROUP_W)),
            pl.BlockSpec((None, 2, nslc, SLC_QB), lambda b, i: (b, 0, 0, i)),
            pl.BlockSpec((None, seq, 128), lambda b, i: (b, 0, COL['b_ks'] // 128)),
            pl.BlockSpec((None, seq, 128), lambda b, i: (b, 0, COL['b_vs'] // 128)),
        ],
        out_specs=pl.BlockSpec((None, SLC_QB, GROUP_W), lambda b, i: (b, i, 0)),
        scratch_shapes=[pltpu.VMEM((2, seq, 128), BF16), pltpu.VMEM((2, seq // SLC_KC, 128, SLC_KC), BF16),
                        pltpu.VMEM((2, 8, 2 * SLC_QB), F32), pltpu.VMEM((2, 128, 2 * SLC_QB), F32)],
        compiler_params=_params(2),
        name="nsa_slc",
    )(u3, sb, u3, u3).reshape(bsz * seq, GROUP_W)


GLA_TC = 1024
_GLA_NLEV = 6


def _gla_constants():
    c = GLA_CHUNK
    idx = np.arange(c)
    t = idx[None, :]
    mats = [(t <= idx[:, None])]
    masks = []
    for lev in range(_GLA_NLEV):
        m = (c // 2) >> lev
        blk = idx // m
        ref_q = blk * m
        ref_k = (blk + 1) * m
        odd = (blk % 2 == 1)
        dq = (t > ref_q[:, None]) & (t <= idx[:, None]) & odd[:, None]
        dk = (t > idx[:, None]) & (t <= ref_k[:, None]) & (~odd)[:, None]
        mats.append(dq | dk)
        same_parent = (idx[:, None] // (2 * m)) == (idx[None, :] // (2 * m))
        masks.append(same_parent & odd[:, None] & (~odd)[None, :])
    masks.append(np.eye(c, dtype=bool))
    dall = np.concatenate(mats, axis=0).astype(np.float32)
    mask4 = np.stack([np.tile(mk.astype(np.float32), (1, N_HEADS)) for mk in masks])
    hv = np.arange(GROUP_W) // HEAD_DIM
    hd = np.arange(N_HEADS * GLA_DK) // GLA_DK
    bd = (hv[:, None] == hd[None, :]).astype(np.float32)
    same_head = (hv[:, None] == hv[None, :]).astype(np.float32)
    return dall, mask4, bd, same_head


def _gla_body(cq_ref, ck_ref, cv_ref, misc_ref, wa_ref, ba_ref, dall_ref, mask_ref, bd_ref,
              sh_ref, gn_ref, o_ref, st):
    j = pl.program_id(1)

    @pl.when(j == 0)
    def _():
        st[...] = jnp.zeros_like(st)

    c = GLA_CHUNK
    nk = N_HEADS * GLA_DK
    lane_h = lax.broadcasted_iota(jnp.int32, (1, nk), 1) // GLA_DK
    lane_v = lax.broadcasted_iota(jnp.int32, (1, GROUP_W), 1) // HEAD_DIM
    dall = dall_ref[...]
    nch = GLA_TC // c
    rows = [slice(ch * c, (ch + 1) * c) for ch in range(nch)]
    z = _dot(misc_ref[...], wa_ref[...]) + ba_ref[...]
    la = (jnp.minimum(z, 0.0) - jnp.log1p(jnp.exp(-jnp.abs(z)))) * (1.0 / GLA_TAU)
    hi, lo = _split_hi_lo(la)
    hl = jnp.concatenate([hi, lo], axis=1)
    e_all = []
    for r in rows:
        e2 = _dot(dall, hl[r])
        e_all.append(e2[:, 0:nk] + e2[:, nk:2 * nk])
    qf = [cq_ref[r, :].astype(F32) * (GLA_DK ** -0.5) for r in rows]
    kf = [ck_ref[r, :].astype(F32) for r in rows]
    vs = [cv_ref[r, :] for r in rows]
    atts, q_in, kv_new, decay = [], [], [], []
    for ch in range(nch):
        bcum = e_all[ch][0:c]
        att = jnp.zeros((c, N_HEADS * c), F32)
        for lev in range(_GLA_NLEV + 1):
            if lev < _GLA_NLEV:
                ex = jnp.exp(e_all[ch][(1 + lev) * c:(2 + lev) * c])
                ql, kl = qf[ch] * ex, kf[ch] * ex
            else:
                ql, kl = qf[ch], kf[ch]
            klb = kl.astype(BF16)
            kbd = jnp.concatenate([jnp.where(lane_h == h, klb, jnp.zeros_like(klb))
                                   for h in range(N_HEADS)], axis=0)
            att = att + _dot_nt(ql.astype(BF16), kbd) * mask_ref[lev]
        atts.append(att.astype(BF16))
        q_in.append((qf[ch] * jnp.exp(bcum)).astype(BF16))
        b_last = bcum[c - 1:c, :]
        kt = (kf[ch] * jnp.exp(b_last - bcum)).astype(BF16)
        kv_new.append(bd_ref[...] * _dot_tn(vs[ch], kt))
        decay.append(jnp.exp(b_last))
    intra = []
    for ch in range(nch):
        vbd = jnp.concatenate([jnp.where(lane_v == h, vs[ch], jnp.zeros_like(vs[ch]))
                               for h in range(N_HEADS)], axis=0)
        intra.append(_dot(atts[ch], vbd))
    state = st[...]
    outs = []
    for ch in range(nch):
        outs.append(intra[ch] + _dot_nt(q_in[ch], state.astype(BF16)))
        state = state * decay[ch] + kv_new[ch]
    st[...] = state
    for ch in range(nch):
        o = outs[ch]
        ms = _dot_hl(o * o, sh_ref[...]) * (1.0 / HEAD_DIM)
        o_ref[rows[ch], :] = (o * lax.rsqrt(ms + EPS) * gn_ref[...]).astype(BF16)


def _gla_call(u3, wa_emb, ba, gnorm, bsz, seq):
    dall, mask4, bd, same_head = _gla_constants()
    tc = GLA_TC
    full = lambda shape: pl.BlockSpec(shape, lambda b, j: (0,) * len(shape))
    return pl.pallas_call(
        _gla_body,
        out_shape=jax.ShapeDtypeStruct((bsz, seq, GROUP_W), BF16),
        grid=(bsz, seq // tc),
        in_specs=[
            pl.BlockSpec((None, tc, 128), lambda b, j: (b, j, COL['c_q'] // 128)),
            pl.BlockSpec((None, tc, 128), lambda b, j: (b, j, COL['c_k'] // 128)),
            pl.BlockSpec((None, tc, GROUP_W), lambda b, j: (b, j, COL['c_v'] // GROUP_W)),
            pl.BlockSpec((None, tc, 128), lambda b, j: (b, j, COL['misc'] // 128)),
            full((128, 128)), full((1, 128)), full(dall.shape), full(mask4.shape),
            full(bd.shape), full(same_head.shape), full((1, GROUP_W)),
        ],
        out_specs=pl.BlockSpec((None, tc, GROUP_W), lambda b, j: (b, j, 0)),
        scratch_shapes=[pltpu.VMEM((GROUP_W, N_HEADS * GLA_DK), F32)],
        compiler_params=_params(2),
        name="gla",
    )(u3, u3, u3, u3, wa_emb, ba, jnp.asarray(dall, BF16), jnp.asarray(mask4, F32),
      jnp.asarray(bd, F32), jnp.asarray(same_head, BF16), gnorm).reshape(bsz * seq, GROUP_W)


OUT_TM = 512


def _gather_residues(src_ref, dst, r, rows, width):
    for j in range(r):
        for s in range(width // 128):
            dst[s, pl.ds(j, rows, stride=r), :] = (
                src_ref[:, j * width + 128 * s:j * width + 128 * (s + 1)].astype(F32))


def _out_body(x_ref, mod_ref, ln_ref, z_ref, o1_ref, o2_ref, o3_ref, l1_ref, l2_ref, l3_ref,
              ocmp_ref, oslc_ref, owin_ref, oc_ref, od_ref, misc_ref, gb_ref, e4_ref, eg_ref,
              w_ref, out_ref, so2, sl2, so3, sl3):
    tm = OUT_TM
    _gather_residues(o2_ref, so2, 4, tm // 4, GROUP_W)
    _gather_residues(l2_ref, sl2, 4, tm // 4, 128)
    _gather_residues(o3_ref, so3, 16, tm // 16, GROUP_W)
    _gather_residues(l3_ref, sl3, 16, tm // 16, 128)
    l1, l2, l3 = l1_ref[...], sl2[0], sl3[0]
    mx = jnp.maximum(jnp.maximum(l1, l2), l3)
    e1, e2, e3 = jnp.exp(l1 - mx), jnp.exp(l2 - mx), jnp.exp(l3 - mx)
    inv = 1.0 / (e1 + e2 + e3)
    e4 = e4_ref[...]
    pair_out = (o1_ref[...].astype(F32),
                jnp.concatenate([so2[0], so2[1]], axis=1),
                jnp.concatenate([so3[0], so3[1]], axis=1))
    oa = jnp.zeros((OUT_TM, GROUP_W), F32)
    for e, o in zip((e1, e2, e3), pair_out):
        oa = oa + _dot_hl2(e * inv, e4) * o
    sg = jax.nn.sigmoid(misc_ref[...].astype(F32) + gb_ref[...])
    sg_hi, sg_lo = _split_hi_lo(sg)
    sg2 = jnp.concatenate([sg_hi, sg_lo], axis=1)
    ob = jnp.zeros((OUT_TM, GROUP_W), F32)
    for br, o_ref in enumerate((ocmp_ref, oslc_ref, owin_ref)):
        ob = ob + _dot(sg2, eg_ref[br]) * o_ref[...].astype(F32)
    z = z_ref[...].astype(F32)
    sz = z * jax.nn.sigmoid(z)
    mixed = jnp.concatenate([oa, ob, oc_ref[...].astype(F32), od_ref[...].astype(F32)], axis=1) * sz
    y = _dot(mixed.astype(BF16), w_ref[...])
    ms = jnp.mean(y * y, axis=-1, keepdims=True)
    yn = y * lax.rsqrt(ms + EPS) * ln_ref[...]
    out_ref[...] = x_ref[...] + mod_ref[2:3, :] * yn


def _out_call(x2, mod3, ln_post, u2, o1, o2, o3, l1, l2, l3, ocmp, oslc, owin, oc, od,
              gate_b_row, e4, eg, w_out, seq):
    t, d = x2.shape
    tm = OUT_TM
    per_b = seq // tm
    row = lambda w: pl.BlockSpec((tm, w), lambda i: (i, 0))
    full = lambda shape: pl.BlockSpec(shape, lambda i: (0,) * len(shape))
    grouped = lambda r, w: pl.BlockSpec((None, tm // r, r * w), lambda i: (i // per_b, i % per_b, 0))
    return pl.pallas_call(
        _out_body,
        out_shape=jax.ShapeDtypeStruct((t, d), F32),
        grid=(t // tm,),
        in_specs=[
            row(d),
            pl.BlockSpec((None, 3, d), lambda i: (i // per_b, 0, 0)),
            full((1, d)),
            pl.BlockSpec((tm, 1024), lambda i: (i, COL['z'] // 1024)),
            row(GROUP_W), grouped(4, GROUP_W), grouped(16, GROUP_W),
            row(128), grouped(4, 128), grouped(16, 128),
            row(GROUP_W), row(GROUP_W), row(GROUP_W), row(GROUP_W), row(GROUP_W),
            pl.BlockSpec((tm, 128), lambda i: (i, COL['misc'] // 128)),
            full((1, 128)), full((256, GROUP_W)), full((3, 256, GROUP_W)), full((d, d)),
        ],
        out_specs=row(d),
        scratch_shapes=[pltpu.VMEM((2, tm, 128), F32), pltpu.VMEM((1, tm, 128), F32),
                        pltpu.VMEM((2, tm, 128), F32), pltpu.VMEM((1, tm, 128), F32)],
        compiler_params=_params(1),
        name="mix_out",
    )(x2, mod3, ln_post.reshape(1, d), u2, o1, o2, o3, l1, l2, l3, ocmp, oslc, owin, oc, od,
      u2, gate_b_row, e4, eg, w_out)


def _permute_columns(w_in):
    pieces, start = [], 0
    src = _SRC_COLS.tolist()
    while start < len(src):
        end = start + 1
        while end < len(src) and ((src[start] < 0 and src[end] < 0) or
                                  (src[start] >= 0 and src[end] == src[end - 1] + 1)):
            end += 1
        if src[start] < 0:
            pieces.append(jnp.zeros(w_in.shape[:-1] + (end - start,), w_in.dtype))
        else:
            pieces.append(w_in[..., src[start]:src[end - 1] + 1])
        start = end
    return jnp.concatenate(pieces, axis=-1)


def _expand_w1(w1):
    w = w1.reshape(2, NSA_CMP_D, HEAD_DIM, NSA_CMP_HID)
    zero = jnp.zeros((NSA_CMP_D, HEAD_DIM, NSA_CMP_HID), w1.dtype)
    top = jnp.concatenate([w[0], w[1], zero, zero], axis=-1)
    bot = jnp.concatenate([zero, zero, w[0], w[1]], axis=-1)
    return jnp.concatenate([top, bot], axis=1).astype(BF16)


def _compress_weights(w1k, w1v):
    wk, wv = _expand_w1(w1k), _expand_w1(w1v)
    zero = jnp.zeros_like(wk)
    rows_k = jnp.concatenate([wk, zero], axis=-1)
    rows_v = jnp.concatenate([zero, wv], axis=-1)
    return jnp.concatenate([rows_k, rows_v], axis=1).reshape(NSA_CMP_D * C_W, 1024)


def _expand_w2(w2):
    z = jnp.zeros_like(w2)
    h0 = jnp.concatenate([w2, w2, z, z], axis=1)
    h1 = jnp.concatenate([z, z, w2, w2], axis=1)
    return jnp.stack([h0, h1]).astype(BF16)


def _static_tables(seq):
    ng = seq // NSA_CMP_D
    nslc = seq // NSA_SLC_L
    cs = np.arange(ng) * NSA_CMP_D
    ss = np.arange(nslc) * NSA_SLC_L
    ovt = ((cs[None, :] < ss[:, None] + NSA_SLC_L) & (cs[None, :] + NSA_CMP_L > ss[:, None]))
    ovt[:, ng - 1] = False
    e4 = np.zeros((128, GROUP_W), np.float32)
    eg = np.zeros((3, 128, GROUP_W), np.float32)
    for h in range(N_HEADS):
        e4[32 * h, HEAD_DIM * h:HEAD_DIM * (h + 1)] = 1.0
        for br in range(3):
            eg[br, MISC_G_OFF + 3 * h + br, HEAD_DIM * h:HEAD_DIM * (h + 1)] = 1.0
    e4 = np.concatenate([e4, e4], axis=0)
    eg = np.concatenate([eg, eg], axis=1)
    return jnp.asarray(ovt, BF16), jnp.asarray(e4, BF16), jnp.asarray(eg, BF16)


def kernel(x, c, ln_pre, ln_post, w_ada, b_ada, w_in, w_out, nsa_pos_k, nsa_pos_v, nsa_w1_k, nsa_b1_k, nsa_w2_k, nsa_w1_v, nsa_b1_v, nsa_w2_v, nsa_gate_b, gla_w_a2, gla_b_a, gla_norm, sinks):
    bsz, seq, d = x.shape
    depth = w_in.shape[0]
    t = bsz * seq

    w_in_r = _permute_columns(w_in).astype(BF16)
    w_out_b = w_out.astype(BF16)
    ovt, e4, eg = _static_tables(seq)

    mod = _ada_call(c, w_ada, b_ada).reshape(depth, bsz, 3, d)

    x2 = x.reshape(t, d)
    for i in range(depth):
        u2, ua, ua4, ua16, uc16 = _proj_in_call(x2, mod[i], ln_pre[i], w_in_r[i], bsz, seq)
        u3 = u2.reshape(bsz, seq, U_W)

        grouped_a = {1: ua.reshape(bsz, seq, A_W), 4: ua4, 16: ua16}
        dil = [_dil_pair_call(grouped_a[r], bsz, seq, w, r) for w, r in DIL_PAIRS]

        kc_rep, vc_rep = _compress_call(
            uc16, bsz, seq, _compress_weights(nsa_w1_k[i], nsa_w1_v[i]),
            jnp.broadcast_to(nsa_pos_k[i].reshape(1, -1), (8, NSA_CMP_L * HEAD_DIM)).astype(BF16),
            jnp.broadcast_to(nsa_pos_v[i].reshape(1, -1), (8, NSA_CMP_L * HEAD_DIM)).astype(BF16),
            nsa_w1_k[i].astype(BF16), nsa_w1_v[i].astype(BF16),
            nsa_b1_k[i].reshape(1, -1), nsa_b1_v[i].reshape(1, -1),
            _expand_w2(nsa_w2_k[i]), _expand_w2(nsa_w2_v[i]))
        ocmp, sb = _cmp_sel_call(u3, kc_rep, vc_rep, ovt, bsz, seq)
        oslc = _slc_call(u3, sb, bsz, seq)
        owin = _win_call(u3, bsz, seq)

        wa_emb = jnp.zeros((128, 128), F32).at[MISC_A_OFF:MISC_A_OFF + GLA_RANK, :].set(gla_w_a2[i]).astype(BF16)
        oc = _gla_call(u3, wa_emb, gla_b_a[i].reshape(1, -1), gla_norm[i].reshape(1, -1), bsz, seq)

        od = _swa_call(u3, sinks[i], bsz, seq)

        gate_b_row = jnp.zeros((1, 128), F32).at[0, MISC_G_OFF:MISC_G_OFF + 3 * N_HEADS].set(nsa_gate_b[i])
        x2 = _out_call(x2, mod[i], ln_post[i], u2, dil[0][0].reshape(t, GROUP_W), dil[1][0], dil[2][0],
                       dil[0][1].reshape(t, 128), dil[1][1], dil[2][1], ocmp.reshape(t, GROUP_W), oslc, owin, oc, od,
                       gate_b_row, e4, eg, w_out_b[i], seq)
    return x2.reshape(bsz, seq, d)
```
